```python
import math, functools
import jax
import jax.numpy as jnp
from jax import lax
import numpy as np

D_MODEL = 2048
BATCH = 2
SEQ = 8192
DEPTH = 4

GRID_W = 64
CTX_LEN = 256
HA = 4
DKA = 128
DVA = 128
HB = 4
DKB = 128
DVB = 128
HC = 8
DHC = 64
DVC = 128
D_MIX = HA * DVA + HB * DVB + HC * DVC
CHUNK = 64
Q_BLOCK = 128
D_FF = 5632
N_EXPERTS = 8
TOP_K = 2
ROPE_BASE = 10000.0
EPS = 1e-6
PROJ_SIZES = (HA * DKA, HA * DKA, HA * DKA, HA * DVA, HA * DVA, HB * DKB, HB * DKB, HB * DVB, HB * DVB, HC * 2 * DHC, HC * 2 * DHC, HC * DVC)
D_PROJ = sum(PROJ_SIZES)
F32 = jnp.float32

kernel_name = 'hybrid_hgrn2_retention_diffattn_moe_dit'


def rmsnorm(x, g):
    xf = x.astype(F32)
    y = xf * lax.rsqrt(jnp.mean(xf * xf, axis=-1, keepdims=True) + EPS)
    return (y * g.astype(F32)).astype(x.dtype)


def head_rmsnorm(o, g, n_heads):
    b, t, w = o.shape
    y = rmsnorm(o.reshape(b, t, n_heads, w // n_heads), g.reshape(n_heads, w // n_heads))
    return y.reshape(b, t, w)


def modulate(x, g, shift, scale):
    return rmsnorm(x, g) * (1 + scale[:, None]) + shift[:, None]


def to_heads(x, n_heads):
    b, t, w = x.shape
    return x.reshape(b, t, n_heads, w // n_heads).transpose(0, 2, 1, 3)


def from_heads(x):
    b, h, t, d = x.shape
    return x.transpose(0, 2, 1, 3).reshape(b, t, h * d)


def split_blocks(x, size):
    b, h, t, d = x.shape
    return jnp.moveaxis(x.reshape(b, h, t // size, size, d), 2, 0)


def merge_blocks(x):
    n, b, h, size, d = x.shape
    return jnp.moveaxis(x, 0, 2).reshape(b, h, n * size, d)


def grid_positions(n_tokens):
    rows = n_tokens // GRID_W
    row = jnp.broadcast_to(jnp.arange(rows, dtype=jnp.int32)[:, None], (rows, GRID_W)).reshape(n_tokens)
    col = jnp.broadcast_to(jnp.arange(GRID_W, dtype=jnp.int32)[None, :], (rows, GRID_W)).reshape(n_tokens)
    return row, col


def rotate_axis(x, pos):
    nf = x.shape[-1] // 2
    inv_freq = ROPE_BASE ** (-jnp.arange(nf, dtype=F32) / nf)
    ang = pos.astype(F32)[:, None] * inv_freq
    cos = jnp.cos(ang).astype(x.dtype)
    sin = jnp.sin(ang).astype(x.dtype)
    x1, x2 = x[..., :nf], x[..., nf:]
    return jnp.concatenate([x1 * cos - x2 * sin, x2 * cos + x1 * sin], axis=-1)


def rope_2d(x, row, col):
    half = x.shape[-1] // 2
    return jnp.concatenate([rotate_axis(x[..., :half], row), rotate_axis(x[..., half:], col)], axis=-1)


def gla_scan(q, k, v, logf, s0, with_out):
    causal = jnp.tril(jnp.ones((CHUNK, CHUNK), dtype=bool))

    def step(s, inp):
        qc, kc, vc, gc = inp
        b = jnp.cumsum(gc, axis=-2)
        b_end = b[..., -1:, :]
        s_new = jnp.exp(b_end[..., 0, :])[..., None] * s + jnp.einsum('bhsk,bhsv->bhkv', kc * jnp.exp(b_end - b), vc)
        if not with_out:
            return s_new, None
        inter = jnp.einsum('bhtk,bhkv->bhtv', qc * jnp.exp(b), s)
        rel = b[..., :, None, :] - b[..., None, :, :]
        decay = jnp.exp(jnp.where(causal[:, :, None], rel, -jnp.inf))
        att = jnp.einsum('bhtk,bhsk,bhtsk->bhts', qc, kc, decay)
        intra = jnp.einsum('bhts,bhsv->bhtv', att, vc)
        return s_new, inter + intra

    xs = (split_blocks(q, CHUNK), split_blocks(k, CHUNK), split_blocks(v, CHUNK), split_blocks(logf, CHUNK))
    s_fin, out = lax.scan(step, s0, xs)
    return (merge_blocks(out) if with_out else None), s_fin


def retention_scan(q, k, v, s0, with_out, log_gamma):
    b, h, t, dk = q.shape
    dv = v.shape[-1]
    n = t // CHUNK
    pos = jnp.arange(CHUNK, dtype=F32)
    lg = log_gamma[:, None]
    qc = q.reshape(b, h, n, CHUNK, dk)
    kc = k.reshape(b, h, n, CHUNK, dk)
    vc = v.reshape(b, h, n, CHUNK, dv)
    kv = jnp.einsum('bhnsk,bhnsv->bhnkv', kc * jnp.exp((CHUNK - 1 - pos) * lg)[None, :, None, :, None], vc)
    chunk_decay = jnp.exp(CHUNK * log_gamma)[None, :, None, None]

    def step(s, kv_j):
        return chunk_decay * s + kv_j, s

    s_fin, s_start = lax.scan(step, s0, jnp.moveaxis(kv, 2, 0))
    if not with_out:
        return None, s_fin
    s_start = jnp.moveaxis(s_start, 0, 2)
    rel = pos[:, None] - pos[None, :]
    dmat = jnp.exp(jnp.where(rel >= 0, rel * log_gamma[:, None, None], -jnp.inf))
    att = jnp.einsum('bhntk,bhnsk->bhnts', qc, kc) * dmat[None, :, None]
    intra = jnp.einsum('bhnts,bhnsv->bhntv', att, vc)
    inter = jnp.einsum('bhntk,bhnkv->bhntv', qc * jnp.exp((pos + 1) * lg)[None, :, None, :, None], s_start)
    return (intra + inter).reshape(b, h, t, dv), s_fin


def bidirectional(scan_fns, ctx_dirs, lat_dirs, with_ctx_out):
    out_c, out_l = None, None
    for d in range(2):
        flip = (lambda a: jnp.flip(a, axis=2)) if d == 1 else (lambda a: a)
        cargs = [flip(a) for a in ctx_dirs[d]]
        largs = [flip(a) for a in lat_dirs[d]]
        b, h, _, dk = cargs[1].shape
        dv = cargs[2].shape[-1]
        s0 = jnp.zeros((b, h, dk, dv), F32)
        oc, sc = scan_fns[d](*cargs, s0, with_ctx_out)
        ol, _ = scan_fns[d](*largs, sc, True)
        out_l = flip(ol) if out_l is None else out_l + flip(ol)
        if with_ctx_out:
            out_c = flip(oc) if out_c is None else out_c + flip(oc)
    return out_c, out_l


def hgrn2_inputs(q, f_fwd, f_bwd, i, lb):
    qh = to_heads(q.astype(F32), HA)
    ih = to_heads(i.astype(F32), HA)
    dirs = []
    for z, lbd in ((f_fwd, lb[0]), (f_bwd, lb[1])):
        logf = jnp.logaddexp(jnp.log(lbd), jnp.log1p(-lbd) + jax.nn.log_sigmoid(z.astype(F32)))
        logf = to_heads(logf, HA)
        dirs.append((qh, -jnp.expm1(logf), ih, logf))
    return dirs


def retention_inputs(q, k, v, rope):
    qh = to_heads(q.astype(F32), HB)
    kh = to_heads(k.astype(F32), HB) * DKB ** -0.5
    vh = to_heads(v.astype(F32), HB)
    if rope is not None:
        qh = rope_2d(qh, *rope)
        kh = rope_2d(kh, *rope)
    return [(qh, kh, vh), (qh, kh, vh)]


def diff_inputs(q, k, v, rope):
    b, t, _ = q.shape
    qh = q.reshape(b, t, HC, 2, DHC).transpose(0, 2, 3, 1, 4)
    kh = k.reshape(b, t, HC, 2, DHC).transpose(0, 2, 3, 1, 4)
    if rope is not None:
        qh = rope_2d(qh, *rope)
        kh = rope_2d(kh, *rope)
    return qh[:, :, 0], qh[:, :, 1], kh[:, :, 0], kh[:, :, 1], to_heads(v, HC)


def diff_attention(q1, q2, k1, k2, v, lam):
    scale = DHC ** -0.5
    s1 = jnp.einsum('bhqd,bhkd->bhqk', q1, k1, preferred_element_type=F32) * scale
    s2 = jnp.einsum('bhqd,bhkd->bhqk', q2, k2, preferred_element_type=F32) * scale
    p = jax.nn.softmax(s1, axis=-1) - lam * jax.nn.softmax(s2, axis=-1)
    return jnp.einsum('bhqk,bhkv->bhqv', p.astype(v.dtype), v)


def merge_head_groups(oa, ob, oc, ga, gb, g_a, g_b, g_c, lam_init, dtype):
    ya = head_rmsnorm(from_heads(oa), g_a, HA) * jax.nn.silu(ga.astype(F32))
    yb = head_rmsnorm(from_heads(ob), g_b, HB) * jax.nn.silu(gb.astype(F32))
    yc = head_rmsnorm(from_heads(oc), g_c, HC).astype(F32) * (1 - lam_init)
    return jnp.concatenate([ya, yb, yc], axis=-1).astype(dtype)


def token_mixers(p_ctx, p_lat, lb, log_gamma, lam, lam_init, g_a, g_b, g_c, row, col, with_ctx_out):
    idx = [int(s) for s in np.cumsum(PROJ_SIZES)[:-1]]
    qa_c, ff_c, fb_c, ia_c, ga_c, qb_c, kb_c, vb_c, gb_c, qd_c, kd_c, vd_c = jnp.split(p_ctx, idx, axis=-1)
    qa_l, ff_l, fb_l, ia_l, ga_l, qb_l, kb_l, vb_l, gb_l, qd_l, kd_l, vd_l = jnp.split(p_lat, idx, axis=-1)

    oa_c, oa_l = bidirectional((gla_scan, gla_scan), hgrn2_inputs(qa_c, ff_c, fb_c, ia_c, lb), hgrn2_inputs(qa_l, ff_l, fb_l, ia_l, lb), with_ctx_out)

    ret_fns = (functools.partial(retention_scan, log_gamma=log_gamma[0]), functools.partial(retention_scan, log_gamma=log_gamma[1]))
    ob_c, ob_l = bidirectional(ret_fns, retention_inputs(qb_c, kb_c, vb_c, None), retention_inputs(qb_l, kb_l, vb_l, (row, col)), with_ctx_out)

    q1c, q2c, k1c, k2c, vc = diff_inputs(qd_c, kd_c, vd_c, None)
    q1l, q2l, k1l, k2l, vl = diff_inputs(qd_l, kd_l, vd_l, (row, col))
    k1_all = jnp.concatenate([k1c, k1l], axis=2)
    k2_all = jnp.concatenate([k2c, k2l], axis=2)
    v_all = jnp.concatenate([vc, vl], axis=2)
    oc_l = merge_blocks(lax.map(lambda qb: diff_attention(qb[0], qb[1], k1_all, k2_all, v_all, lam), (split_blocks(q1l, Q_BLOCK), split_blocks(q2l, Q_BLOCK))))

    y_lat = merge_head_groups(oa_l, ob_l, oc_l, ga_l, gb_l, g_a, g_b, g_c, lam_init, p_lat.dtype)
    if not with_ctx_out:
        return None, y_lat
    oc_c = diff_attention(q1c, q2c, k1c, k2c, vc, lam)
    y_ctx = merge_head_groups(oa_c, ob_c, oc_c, ga_c, gb_c, g_a, g_b, g_c, lam_init, p_ctx.dtype)
    return y_ctx, y_lat


def swiglu(h, w1, w3, w2):
    return (jax.nn.silu(h @ w1) * (h @ w3)) @ w2


def moe(h, w_router, w1, w3, w2):
    logits = (h @ w_router).astype(F32)
    top_v, top_i = lax.top_k(logits, TOP_K)
    wts = jax.nn.softmax(top_v, axis=-1)
    gates = jnp.sum(jax.nn.one_hot(top_i, N_EXPERTS, dtype=F32) * wts[..., None], axis=-2)
    y = jnp.zeros_like(h)
    for e in range(N_EXPERTS):
        y = y + gates[..., e:e + 1].astype(h.dtype) * swiglu(h, w1[e], w3[e], w2[e])
    return y


def setup_inputs(seed: int = 0) -> dict:
    key = jax.random.key(seed)
    ks = jax.random.split(key, 24)
    D, L, F = D_MODEL, DEPTH, D_FF
    n_dense = (DEPTH + 1) // 2
    n_moe = DEPTH // 2

    def nrm(k, shape, scale):
        return jax.random.normal(k, shape, F32) * scale

    return {
        'x': nrm(ks[0], (BATCH, SEQ, D), 1.0),
        'c': nrm(ks[1], (BATCH, D), 1.0),
        'ctx': nrm(ks[2], (BATCH, CTX_LEN, D), 1.0),
        'c_ctx': nrm(ks[3], (D,), 1.0),
        'w_ada': nrm(ks[4], (L, D, 6 * D), 0.5 * D ** -0.5),
        'b_ada': nrm(ks[5], (L, 6 * D), 0.02),
        'norm_mix': 1.0 + nrm(ks[6], (L, D), 0.05),
        'w_in': nrm(ks[7], (L, D, D_PROJ), D ** -0.5),
        'hgrn_lower_bounds': nrm(ks[8], (L, 2, HA * DKA), 0.5),
        'ret_decay': 5.0 + jnp.arange(HB, dtype=F32) + nrm(ks[9], (L, 2, HB), 0.1),
        'diff_lambda': nrm(ks[10], (L, 4, DHC), 0.1),
        'norm_a': 1.0 + nrm(ks[11], (L, HA * DVA), 0.05),
        'norm_b': 1.0 + nrm(ks[12], (L, HB * DVB), 0.05),
        'norm_c': 1.0 + nrm(ks[13], (L, HC * DVC), 0.05),
        'w_out': nrm(ks[14], (L, D_MIX, D), D_MIX ** -0.5),
        'norm_ffn': 1.0 + nrm(ks[15], (L, D), 0.05),
        'ffn_w1': nrm(ks[16], (n_dense, D, F), D ** -0.5),
        'ffn_w3': nrm(ks[17], (n_dense, D, F), D ** -0.5),
        'ffn_w2': nrm(ks[18], (n_dense, F, D), F ** -0.5),
        'router': nrm(ks[19], (n_moe, D, N_EXPERTS), D ** -0.5),
        'moe_w1': nrm(ks[20], (n_moe, N_EXPERTS, D, F), D ** -0.5),
        'moe_w3': nrm(ks[21], (n_moe, N_EXPERTS, D, F), D ** -0.5),
        'moe_w2': nrm(ks[22], (n_moe, N_EXPERTS, F, D), F ** -0.5),
        'final_norm': 1.0 + nrm(ks[23], (D,), 0.05),
    }


def reference(x, c, ctx, c_ctx, w_ada, b_ada, norm_mix, w_in, hgrn_lower_bounds, ret_decay, diff_lambda, norm_a, norm_b, norm_c, w_out, norm_ffn, ffn_w1, ffn_w3, ffn_w2, router, moe_w1, moe_w3, moe_w2, final_norm):
    n_tokens = x.shape[1]
    row, col = grid_positions(n_tokens)
    lbs = jnp.cumsum(jax.nn.softmax(hgrn_lower_bounds.astype(F32), axis=0), axis=0)
    lbs = lbs - lbs[0:1]
    log_gammas = jnp.log1p(-jnp.exp2(-ret_decay.astype(F32)))
    sc = jax.nn.silu(c)
    sc_ctx = jax.nn.silu(c_ctx)[None]
    x_ctx = ctx
    for l in range(DEPTH):
        need_ctx = l < DEPTH - 1
        sh1, s1, g1, sh2, s2, g2 = jnp.split(sc @ w_ada[l] + b_ada[l], 6, axis=-1)
        ch1, cs1, cg1, ch2, cs2, cg2 = jnp.split(sc_ctx @ w_ada[l] + b_ada[l], 6, axis=-1)
        lam_init = 0.8 - 0.6 * math.exp(-0.3 * l)
        lq1, lk1, lq2, lk2 = diff_lambda[l].astype(F32)
        lam = jnp.exp(jnp.sum(lq1 * lk1)) - jnp.exp(jnp.sum(lq2 * lk2)) + lam_init

        h = modulate(x, norm_mix[l], sh1, s1)
        h_ctx = modulate(x_ctx, norm_mix[l], ch1, cs1)
        y_ctx, y = token_mixers(h_ctx @ w_in[l], h @ w_in[l], lbs[l], log_gammas[l], lam, lam_init, norm_a[l], norm_b[l], norm_c[l], row, col, need_ctx)
        x = x + g1[:, None] * (y @ w_out[l])
        if need_ctx:
            x_ctx = x_ctx + cg1[:, None] * (y_ctx @ w_out[l])

        if l % 2 == 0:
            ffn = functools.partial(swiglu, w1=ffn_w1[l // 2], w3=ffn_w3[l // 2], w2=ffn_w2[l // 2])
        else:
            ffn = functools.partial(moe, w_router=router[l // 2], w1=moe_w1[l // 2], w3=moe_w3[l // 2], w2=moe_w2[l // 2])
        x = x + g2[:, None] * ffn(modulate(x, norm_ffn[l], sh2, s2))
        if need_ctx:
            x_ctx = x_ctx + cg2[:, None] * ffn(modulate(x_ctx, norm_ffn[l], ch2, cs2))
    return rmsnorm(x, final_norm)
```

```python
import functools
import math

import numpy as np
import jax
import jax.numpy as jnp
from jax import lax
from jax.experimental import pallas as pl
from jax.experimental.pallas import tpu as pltpu

F32 = jnp.float32
BF16 = jnp.bfloat16

HA, DKA, DVA = 4, 128, 128
HB, DKB, DVB = 4, 128, 128
HC, DHC, DVC = 8, 64, 128
GRID_W = 64
N_EXPERTS = 8
ROPE_BASE = 10000.0
EPS = 1e-6
WA = HA * DKA
D_PROJ = 5 * WA + 4 * WA + 3 * HC * DVC
D_MIX = HA * DVA + HB * DVB + HC * DVC
COL_QA, COL_FF, COL_FB, COL_IA, COL_GA, COL_QB, COL_KB, COL_VB, COL_GB = range(9)
COL_QD, COL_KD, COL_VD = 9, 11, 13

LANES = 128
SUBLANES = 8
VMEM_LIMIT_MB = 56

CHUNK_A = 64
LEVELS_A = (32, 16, 8)
CHUNK_B = 256
NEG_BIG = -1e30


def _cparams(semantics, vmem_mb=VMEM_LIMIT_MB):
    return pltpu.CompilerParams(dimension_semantics=semantics, vmem_limit_bytes=vmem_mb * 2 ** 20)


def _pick(n, candidates):
    for c in candidates:
        if n % c == 0:
            return c
    raise ValueError(f"no block size for {n} in {candidates}")


def _split3(x):
    hi = x.astype(BF16)
    r1 = x - hi.astype(F32)
    mid = r1.astype(BF16)
    lo = (r1 - mid.astype(F32)).astype(BF16)
    return hi, mid, lo


def _dot(a, b):
    return jnp.dot(a, b, preferred_element_type=F32)


def _dot_nt(a, b):
    return lax.dot_general(a, b, (((1,), (1,)), ((), ())), preferred_element_type=F32)


def _dot_tn(a, b):
    return lax.dot_general(a, b, (((0,), (0,)), ((), ())), preferred_element_type=F32)


def _ada_kernel(c_ref, w_ref, b_ref, o_ref):
    c = c_ref[...]
    s = c * jax.nn.sigmoid(c)
    o_ref[0] = _dot(s.astype(BF16), w_ref[0].astype(BF16)) + b_ref[0]


def _ada_all(cvec, w_ada, b_ada):
    L, D, N = w_ada.shape
    bn = _pick(N, (1024, 768, 512, 256, 128))
    return pl.pallas_call(
        _ada_kernel,
        grid=(L, N // bn),
        in_specs=[
            pl.BlockSpec((SUBLANES, D), lambda l, j: (0, 0)),
            pl.BlockSpec((1, D, bn), lambda l, j: (l, 0, j)),
            pl.BlockSpec((1, 1, bn), lambda l, j: (l, 0, j)),
        ],
        out_specs=pl.BlockSpec((1, SUBLANES, bn), lambda l, j: (l, 0, j)),
        out_shape=jax.ShapeDtypeStruct((L, SUBLANES, N), F32),
        compiler_params=_cparams(("arbitrary", "arbitrary")),
    )(cvec, w_ada, b_ada.reshape(L, 1, N))


def _rms(x):
    return x * lax.rsqrt(jnp.mean(x * x, axis=-1, keepdims=True) + EPS)


def _modulate_kernel(x_ref, g_ref, sh_ref, sc_ref, o_ref):
    y = _rms(x_ref[0]) * g_ref[...]
    o_ref[0] = (y * (1 + sc_ref[0]) + sh_ref[0]).astype(o_ref.dtype)


def _modulate_route_kernel(x_ref, g_ref, sh_ref, sc_ref, wr_ref, o_ref, gate_ref):
    y = _rms(x_ref[0]) * g_ref[...]
    h = y * (1 + sc_ref[0]) + sh_ref[0]
    o_ref[0] = h.astype(o_ref.dtype)
    hi = h.astype(BF16)
    lo = (h - hi.astype(F32)).astype(BF16)
    logits = _dot(hi, wr_ref[0]) + (_dot(lo, wr_ref[0]) + _dot(hi, wr_ref[1]))
    lane = lax.broadcasted_iota(jnp.int32, logits.shape, 1)
    logits = jnp.where(lane < N_EXPERTS, logits, -jnp.inf)
    m1 = jnp.max(logits, axis=-1, keepdims=True)
    i1 = jnp.min(jnp.where(logits == m1, lane, LANES), axis=-1, keepdims=True)
    rest = jnp.where(lane == i1, -jnp.inf, logits)
    m2 = jnp.max(rest, axis=-1, keepdims=True)
    i2 = jnp.min(jnp.where(rest == m2, lane, LANES), axis=-1, keepdims=True)
    e2 = jnp.exp(m2 - m1)
    w1 = 1.0 / (1.0 + e2)
    gate_ref[0] = jnp.where(lane == i1, w1, 0.0) + jnp.where(lane == i2, e2 * w1, 0.0)


def _mod_index(base, nctx_blocks):
    def index(b, j, *, nb):
        return (base + jnp.where(j < nctx_blocks, nb, b), 0, 0)
    return index


def _modulate(xc, g, mods, base_shift, base_scale, ctx, out_dtype=BF16, router=None, row_offset=0, rows=None):
    B, TT, D = xc.shape
    bt = _pick(ctx, (256, 128, 64))
    rows = TT if rows is None else rows
    off = row_offset // bt
    nctx = max(ctx // bt - off, 0)
    sh_idx = functools.partial(_mod_index(base_shift, nctx), nb=B)
    sc_idx = functools.partial(_mod_index(base_scale, nctx), nb=B)
    in_specs = [
        pl.BlockSpec((1, bt, D), lambda b, j: (b, j + off, 0)),
        pl.BlockSpec((1, D), lambda b, j: (0, 0)),
        pl.BlockSpec((1, 1, D), sh_idx),
        pl.BlockSpec((1, 1, D), sc_idx),
    ]
    out_spec = pl.BlockSpec((1, bt, D), lambda b, j: (b, j, 0))
    out_shape = jax.ShapeDtypeStruct((B, rows, D), out_dtype)
    if router is None:
        return pl.pallas_call(
            _modulate_kernel, grid=(B, rows // bt), in_specs=in_specs, out_specs=out_spec, out_shape=out_shape,
            compiler_params=_cparams(("arbitrary", "arbitrary")),
        )(xc, g.reshape(1, D), mods, mods)
    in_specs.append(pl.BlockSpec((2, D, LANES), lambda b, j: (0, 0, 0)))
    return pl.pallas_call(
        _modulate_route_kernel, grid=(B, rows // bt), in_specs=in_specs,
        out_specs=[out_spec, pl.BlockSpec((1, bt, LANES), lambda b, j: (b, j, 0))],
        out_shape=[out_shape, jax.ShapeDtypeStruct((B, rows, LANES), F32)],
        compiler_params=_cparams(("arbitrary", "arbitrary")),
    )(xc, g.reshape(1, D), mods, mods, router)


def _mm_kernel(a_ref, w_ref, o_ref):
    o_ref[...] = _dot(a_ref[...], w_ref[...]).astype(o_ref.dtype)


def _matmul(a, w, out_dtype, bm, bn):
    M, K = a.shape
    N = w.shape[1]
    return pl.pallas_call(
        _mm_kernel, grid=(M // bm, N // bn),
        in_specs=[pl.BlockSpec((bm, K), lambda i, j: (i, 0)), pl.BlockSpec((K, bn), lambda i, j: (0, j))],
        out_specs=pl.BlockSpec((bm, bn), lambda i, j: (i, j)),
        out_shape=jax.ShapeDtypeStruct((M, N), out_dtype),
        compiler_params=_cparams(("arbitrary", "arbitrary")),
    )(a, w)


def _row_gate(shape, i, nb_per_batch, ctx, g_ctx, g_lat):
    nctx = jnp.where(i % nb_per_batch == 0, ctx, 0)
    rows = lax.broadcasted_iota(jnp.int32, shape, 0)
    return jnp.where(rows < nctx, g_ctx, g_lat)


def _mm_res_kernel(a_ref, w_ref, x_ref, gl_ref, gc_ref, o_ref, *, nb_per_batch, ctx):
    acc = _dot(a_ref[...], w_ref[...])
    g = _row_gate(acc.shape, pl.program_id(0), nb_per_batch, ctx, gc_ref[0], gl_ref[0])
    o_ref[...] = x_ref[...] + g * acc


def _matmul_residual(a, w, x2, mods, base_gate, B, ctx, bm, bn):
    M, K = a.shape
    N = w.shape[1]
    nbb = (M // B) // bm
    return pl.pallas_call(
        functools.partial(_mm_res_kernel, nb_per_batch=nbb, ctx=ctx),
        grid=(M // bm, N // bn),
        in_specs=[
            pl.BlockSpec((bm, K), lambda i, j: (i, 0)),
            pl.BlockSpec((K, bn), lambda i, j: (0, j)),
            pl.BlockSpec((bm, bn), lambda i, j: (i, j)),
            pl.BlockSpec((1, 1, bn), lambda i, j: (base_gate + i // nbb, 0, j)),
            pl.BlockSpec((1, 1, bn), lambda i, j: (base_gate + B, 0, j)),
        ],
        out_specs=pl.BlockSpec((bm, bn), lambda i, j: (i, j)),
        out_shape=jax.ShapeDtypeStruct((M, N), F32),
        compiler_params=_cparams(("arbitrary", "arbitrary")),
    )(a, w, x2, mods, mods)


def _silu(x):
    return x * jax.nn.sigmoid(x)


def _ffn_kernel(h_ref, w1_ref, w3_ref, w2_ref, x_ref, gl_ref, gc_ref, o_ref, *, nb_per_batch, ctx, nf):
    f = pl.program_id(1)
    h = h_ref[...]
    hid = (_silu(_dot(h, w1_ref[...])) * _dot(h, w3_ref[...])).astype(BF16)
    part = _dot(hid, w2_ref[...])

    @pl.when(f == 0)
    def _():
        o_ref[...] = part

    @pl.when(f > 0)
    def _():
        o_ref[...] += part

    @pl.when(f == nf - 1)
    def _():
        g = _row_gate(o_ref.shape, pl.program_id(0), nb_per_batch, ctx, gc_ref[0], gl_ref[0])
        o_ref[...] = x_ref[...] + g * o_ref[...]


def _ffn(h, w1, w3, w2, x2, mods, base_gate, B, ctx, bm, bf):
    M, D = h.shape
    F = w1.shape[1]
    nf = F // bf
    nbb = (M // B) // bm
    return pl.pallas_call(
        functools.partial(_ffn_kernel, nb_per_batch=nbb, ctx=ctx, nf=nf),
        grid=(M // bm, nf),
        in_specs=[
            pl.BlockSpec((bm, D), lambda i, f: (i, 0)),
            pl.BlockSpec((D, bf), lambda i, f: (0, f)),
            pl.BlockSpec((D, bf), lambda i, f: (0, f)),
            pl.BlockSpec((bf, D), lambda i, f: (f, 0)),
            pl.BlockSpec((bm, D), lambda i, f: (i, 0)),
            pl.BlockSpec((1, 1, D), lambda i, f: (base_gate + i // nbb, 0, 0)),
            pl.BlockSpec((1, 1, D), lambda i, f: (base_gate + B, 0, 0)),
        ],
        out_specs=pl.BlockSpec((bm, D), lambda i, f: (i, 0)),
        out_shape=jax.ShapeDtypeStruct((M, D), F32),
        compiler_params=_cparams(("arbitrary", "arbitrary")),
    )(h, w1, w3, w2, x2, mods, mods)


def _moe_kernel(h_ref, gate_ref, w1_ref, w3_ref, w2_ref, x_ref, gl_ref, gc_ref, o_ref, *, nb_per_batch, ctx, nf):
    e = pl.program_id(1)
    f = pl.program_id(2)
    h = h_ref[...]
    gates = gate_ref[...]
    lane = lax.broadcasted_iota(jnp.int32, gates.shape, 1)
    ge = jnp.sum(jnp.where(lane == e, gates, 0.0), axis=-1, keepdims=True)
    hid = (_silu(_dot(h, w1_ref[0])) * _dot(h, w3_ref[0]) * ge).astype(BF16)
    part = _dot(hid, w2_ref[0])
    first = (e == 0) & (f == 0)

    @pl.when(first)
    def _():
        o_ref[...] = part

    @pl.when(jnp.logical_not(first))
    def _():
        o_ref[...] += part

    @pl.when((e == N_EXPERTS - 1) & (f == nf - 1))
    def _():
        g = _row_gate(o_ref.shape, pl.program_id(0), nb_per_batch, ctx, gc_ref[0], gl_ref[0])
        o_ref[...] = x_ref[...] + g * o_ref[...]


def _moe_dense(h, gates, w1, w3, w2, x2, mods, base_gate, B, ctx, bm, bf):
    M, D = h.shape
    E, _, F = w1.shape
    nf = F // bf
    nbb = (M // B) // bm
    return pl.pallas_call(
        functools.partial(_moe_kernel, nb_per_batch=nbb, ctx=ctx, nf=nf),
        grid=(M // bm, E, nf),
        in_specs=[
            pl.BlockSpec((bm, D), lambda i, e, f: (i, 0)),
            pl.BlockSpec((bm, LANES), lambda i, e, f: (i, 0)),
            pl.BlockSpec((1, D, bf), lambda i, e, f: (e, 0, f)),
            pl.BlockSpec((1, D, bf), lambda i, e, f: (e, 0, f)),
            pl.BlockSpec((1, bf, D), lambda i, e, f: (e, f, 0)),
            pl.BlockSpec((bm, D), lambda i, e, f: (i, 0)),
            pl.BlockSpec((1, 1, D), lambda i, e, f: (base_gate + i // nbb, 0, 0)),
            pl.BlockSpec((1, 1, D), lambda i, e, f: (base_gate + B, 0, 0)),
        ],
        out_specs=pl.BlockSpec((bm, D), lambda i, e, f: (i, 0)),
        out_shape=jax.ShapeDtypeStruct((M, D), F32),
        compiler_params=_cparams(("arbitrary", "arbitrary", "arbitrary")),
    )(h, gates, w1, w3, w2, x2, mods, mods)


def _rope_tables(seq, ctx, d):
    nf = d // 4
    rows = seq // GRID_W
    row = jnp.broadcast_to(jnp.arange(rows, dtype=jnp.int32)[:, None], (rows, GRID_W)).reshape(seq)
    col = jnp.broadcast_to(jnp.arange(GRID_W, dtype=jnp.int32)[None, :], (rows, GRID_W)).reshape(seq)
    inv_freq = ROPE_BASE ** (-jnp.arange(nf, dtype=F32) / nf)
    ar = row.astype(F32)[:, None] * inv_freq
    ac = col.astype(F32)[:, None] * inv_freq
    cos = jnp.concatenate([jnp.cos(ar), jnp.cos(ar), jnp.cos(ac), jnp.cos(ac)], axis=-1)
    sin = jnp.concatenate([-jnp.sin(ar), jnp.sin(ar), -jnp.sin(ac), jnp.sin(ac)], axis=-1)
    cos = jnp.concatenate([jnp.ones((ctx, d), F32), cos], axis=0)
    sin = jnp.concatenate([jnp.zeros((ctx, d), F32), sin], axis=0)
    return cos, sin


def _rope(x, cos, sin, quarter):
    lane = lax.broadcasted_iota(jnp.int32, x.shape, 1)
    up = pltpu.roll(x, LANES - quarter, 1)
    down = pltpu.roll(x, quarter, 1)
    swapped = jnp.where(lane % (2 * quarter) < quarter, up, down)
    return x * cos + swapped * sin


def _prep_kernel(qb_ref, kb_ref, vb_ref, qd0_ref, qd1_ref, kd0_ref, kd1_ref, vd0_ref, vd1_ref,
                 cb_ref, sb_ref, cc_ref, sc_ref, oqb, okb, ovb, oqd, okd, ovd):
    cb, sb, cc, sc = cb_ref[...], sb_ref[...], cc_ref[...], sc_ref[...]
    qb, kb = qb_ref[0], kb_ref[0]
    for h in range(HB):
        hs = slice(h * LANES, (h + 1) * LANES)
        oqb[0, :, hs] = _rope(qb[:, hs], cb, sb, DKB // 4).astype(BF16)
        okb[0, :, hs] = (_rope(kb[:, hs], cb, sb, DKB // 4) * DKB ** -0.5).astype(BF16)
    ovb[0] = vb_ref[0].astype(BF16)
    for half, (qr, kr, vr) in enumerate(((qd0_ref, kd0_ref, vd0_ref), (qd1_ref, kd1_ref, vd1_ref))):
        q, k = qr[0], kr[0]
        for h in range(WA // LANES):
            hs = slice(h * LANES, (h + 1) * LANES)
            os_ = slice(half * WA + h * LANES, half * WA + (h + 1) * LANES)
            oqd[0, :, os_] = (_rope(q[:, hs], cc, sc, DHC // 4) * DHC ** -0.5).astype(BF16)
            okd[0, :, os_] = _rope(k[:, hs], cc, sc, DHC // 4).astype(BF16)
        ovd[0, :, half * WA:(half + 1) * WA] = vr[0].astype(BF16)


def _prep(p, tabs, ctx):
    B, TT, _ = p.shape
    bt = _pick(ctx, (256, 128, 64))
    pspec = lambda c: pl.BlockSpec((1, bt, WA), lambda b, j: (b, j, c))
    tspec = pl.BlockSpec((bt, LANES), lambda b, j: (j, 0))
    o512 = pl.BlockSpec((1, bt, WA), lambda b, j: (b, j, 0))
    o1024 = pl.BlockSpec((1, bt, 2 * WA), lambda b, j: (b, j, 0))
    s512 = jax.ShapeDtypeStruct((B, TT, WA), BF16)
    s1024 = jax.ShapeDtypeStruct((B, TT, 2 * WA), BF16)
    cols = (COL_QB, COL_KB, COL_VB, COL_QD, COL_QD + 1, COL_KD, COL_KD + 1, COL_VD, COL_VD + 1)
    return pl.pallas_call(
        _prep_kernel, grid=(B, TT // bt),
        in_specs=[pspec(c) for c in cols] + [tspec] * 4,
        out_specs=[o512, o512, o512, o1024, o1024, o1024],
        out_shape=[s512, s512, s512, s1024, s1024, s1024],
        compiler_params=_cparams(("arbitrary", "arbitrary")),
    )(*([p] * 9), *tabs)


def _hgrn_consts():
    C = CHUNK_A
    t = np.arange(C)[:, None]
    u = np.arange(C)[None, :]
    out = []
    for rev in (False, True):
        mats = [(u >= t) if rev else (u <= t)]
        for m in LEVELS_A:
            r = (t // (2 * m)) * (2 * m) + (m if rev else m - 1)
            mats.append((u >= r) if rev else (u <= r))
        out.append(np.concatenate(mats, axis=0))
    return jnp.asarray(np.stack(out).astype(np.float32), dtype=BF16)


def _hgrn_direction(rev, q_ref, z_ref, v_ref, o_ref, cm, llb, l1m, s_scr, k_scr, b_scr, d_scr):
    C = CHUNK_A
    z = z_ref[0]
    log_sig = jnp.minimum(z, 0.0) - jnp.log1p(jnp.exp(-jnp.abs(z)))
    t2 = l1m + log_sig
    logf = jnp.maximum(llb, t2) + jnp.log1p(jnp.exp(-jnp.abs(llb - t2)))
    k_scr[...] = 1.0 - jnp.exp(logf)
    parts = _split3(logf)
    bb = _dot(cm, parts[0]) + (_dot(cm, parts[1]) + _dot(cm, parts[2]))
    b_scr[...] = bb[:C]
    ones = jnp.ones((C, LANES), BF16)
    bend_t = _dot_tn(parts[0], ones) + (_dot_tn(parts[1], ones) + _dot_tn(parts[2], ones))

    row = lax.broadcasted_iota(jnp.int32, (C, LANES), 0)
    rr = lax.broadcasted_iota(jnp.int32, (C, C), 0)
    cc = lax.broadcasted_iota(jnp.int32, (C, C), 1)
    sub_row = lax.broadcasted_iota(jnp.int32, (SUBLANES, LANES), 0)
    end = 0 if rev else C - 1

    def diag_block(blk, carry):
        r0 = pl.multiple_of(blk * SUBLANES, SUBLANES)
        for h in range(HA):
            hs = slice(h * LANES, (h + 1) * LANES)
            qi = q_ref[0, pl.ds(r0, SUBLANES), hs]
            bi = b_scr[pl.ds(r0, SUBLANES), hs]
            ki = k_scr[pl.ds(r0, SUBLANES), hs]
            vi = v_ref[0, pl.ds(r0, SUBLANES), hs]
            acc = jnp.zeros((SUBLANES, LANES), F32)
            for j in range(SUBLANES):
                kj, bj, vj = ki[j:j + 1, :], bi[j:j + 1, :], vi[j:j + 1, :]
                e = jnp.exp(jnp.minimum(bi - bj, 0.0))
                a = jnp.sum(qi * kj * e, axis=-1, keepdims=True)
                valid = (sub_row <= j) if rev else (sub_row >= j)
                acc = acc + jnp.where(valid, a, 0.0) * vj
            d_scr[pl.ds(r0, SUBLANES), hs] = acc
        return carry

    lax.fori_loop(0, C // SUBLANES, diag_block, 0)

    for h in range(HA):
        hs = slice(h * LANES, (h + 1) * LANES)
        q = q_ref[0, :, hs]
        k = k_scr[:, hs]
        v = v_ref[0, :, hs].astype(BF16)
        b = bb[:C, hs]
        s_old = s_scr[h]
        inter = _dot((q * jnp.exp(b)).astype(BF16), s_old.astype(BF16))
        att = jnp.zeros((C, C), F32)
        for lvl, m in enumerate(LEVELS_A):
            bref = bb[(lvl + 1) * C:(lvl + 2) * C, hs]
            e = jnp.exp(-jnp.abs(b - bref))
            upper = (row % (2 * m)) >= m
            q_side = jnp.logical_not(upper) if rev else upper
            qt = jnp.where(q_side, q * e, 0.0).astype(BF16)
            kt = jnp.where(q_side, 0.0, k * e).astype(BF16)
            a = _dot_nt(qt, kt)
            att = att + jnp.where((rr // (2 * m)) == (cc // (2 * m)), a, 0.0)
        o_ref[0, :, hs] = inter + _dot(att.astype(BF16), v) + d_scr[:, hs]
        bend = b[end:end + 1, :]
        kdec = (k * jnp.exp(bend - b)).astype(BF16)
        s_scr[h] = jnp.exp(bend_t[hs, :]) * s_old + _dot_tn(kdec, v)


def _hgrn_kernel(qf_ref, zf_ref, vf_ref, qb_ref, zb_ref, vb_ref, cm_ref, lb_ref, of_ref, ob_ref,
                 s_scr, k_scr, b_scr, d_scr):
    @pl.when(pl.program_id(1) == 0)
    def _():
        s_scr[...] = jnp.zeros_like(s_scr)

    _hgrn_direction(False, qf_ref, zf_ref, vf_ref, of_ref, cm_ref[0], lb_ref[0, 0:1], lb_ref[0, 1:2],
                    s_scr.at[0], k_scr, b_scr, d_scr)
    _hgrn_direction(True, qb_ref, zb_ref, vb_ref, ob_ref, cm_ref[1], lb_ref[1, 0:1], lb_ref[1, 1:2],
                    s_scr.at[1], k_scr, b_scr, d_scr)


def _bwd_chunk(i, nctx, n):
    return jnp.where(i < nctx, nctx - 1 - i, n - 1 + nctx - i)


def _hgrn(p, lb_tab, ctx):
    B, TT, _ = p.shape
    C = CHUNK_A
    n, nctx = TT // C, ctx // C
    fwd = lambda c: pl.BlockSpec((1, C, WA), lambda b, i: (b, i, c))
    bwd = lambda c: pl.BlockSpec((1, C, WA), lambda b, i: (b, _bwd_chunk(i, nctx, n), c))
    cm = _hgrn_consts()
    shape = jax.ShapeDtypeStruct((B, TT, WA), F32)
    return pl.pallas_call(
        _hgrn_kernel, grid=(B, n),
        in_specs=[fwd(COL_QA), fwd(COL_FF), fwd(COL_IA), bwd(COL_QA), bwd(COL_FB), bwd(COL_IA),
                  pl.BlockSpec(cm.shape, lambda b, i: (0, 0, 0)),
                  pl.BlockSpec((2, 2, WA), lambda b, i: (0, 0, 0))],
        out_specs=[pl.BlockSpec((1, C, WA), lambda b, i: (b, i, 0)),
                   pl.BlockSpec((1, C, WA), lambda b, i: (b, _bwd_chunk(i, nctx, n), 0))],
        out_shape=[shape, shape],
        scratch_shapes=[pltpu.VMEM((2, HA, DKA, DVA), F32), pltpu.VMEM((C, WA), F32),
                        pltpu.VMEM((C, WA), F32), pltpu.VMEM((C, WA), F32)],
        compiler_params=_cparams(("arbitrary", "arbitrary")),
    )(p, p, p, p, p, p, cm, lb_tab)


def _ret_kernel(lg_ref, qf_ref, kf_ref, vf_ref, qb_ref, kb_ref, vb_ref, of_ref, ob_ref,
                s_scr, d_scr, rq_scr, rk_scr, gc_scr):
    C = CHUNK_B

    @pl.when((pl.program_id(0) == 0) & (pl.program_id(1) == 0))
    def _():
        t = lax.broadcasted_iota(jnp.int32, (C, C), 0)
        s = lax.broadcasted_iota(jnp.int32, (C, C), 1)
        pos = lax.broadcasted_iota(jnp.int32, (C, LANES), 0).astype(F32)
        for d in range(2):
            for h in range(HB):
                lg = lg_ref[d, h]
                delta = (s - t) if d else (t - s)
                d_scr[d, h] = jnp.where(delta >= 0, jnp.exp(jnp.maximum(delta, 0).astype(F32) * lg), 0.0)
                rq_scr[d, h] = jnp.exp(((C - pos) if d else (pos + 1.0)) * lg)
                rk_scr[d, h] = jnp.exp((pos if d else (C - 1.0 - pos)) * lg)
                gc_scr[d, h] = jnp.exp(jnp.full((SUBLANES, LANES), C, F32) * lg)

    @pl.when(pl.program_id(1) == 0)
    def _():
        s_scr[...] = jnp.zeros_like(s_scr)

    for d, (q_ref, k_ref, v_ref, o_ref) in enumerate(((qf_ref, kf_ref, vf_ref, of_ref), (qb_ref, kb_ref, vb_ref, ob_ref))):
        for h in range(HB):
            hs = slice(h * LANES, (h + 1) * LANES)
            q, k, v = q_ref[0, :, hs], k_ref[0, :, hs], v_ref[0, :, hs]
            att = (_dot_nt(q, k) * d_scr[d, h]).astype(BF16)
            s_old = s_scr[d, h]
            o_ref[0, :, hs] = _dot(att, v) + rq_scr[d, h] * _dot(q, s_old.astype(BF16))
            kdec = (k.astype(F32) * rk_scr[d, h]).astype(BF16)
            s_scr[d, h] = gc_scr[d, h][0:1, :] * s_old + _dot_tn(kdec, v)


def _retention(qb, kb, vb, log_gamma, ctx):
    B, TT, _ = qb.shape
    C = CHUNK_B
    n, nctx = TT // C, ctx // C
    fwd = pl.BlockSpec((1, C, WA), lambda b, i: (b, i, 0))
    bwd = pl.BlockSpec((1, C, WA), lambda b, i: (b, _bwd_chunk(i, nctx, n), 0))
    shape = jax.ShapeDtypeStruct((B, TT, WA), F32)
    return pl.pallas_call(
        _ret_kernel, grid=(B, n),
        in_specs=[pl.BlockSpec(memory_space=pltpu.SMEM), fwd, fwd, fwd, bwd, bwd, bwd],
        out_specs=[fwd, bwd], out_shape=[shape, shape],
        scratch_shapes=[pltpu.VMEM((2, HB, DKB, DVB), F32), pltpu.VMEM((2, HB, C, C), F32),
                        pltpu.VMEM((2, HB, C, LANES), F32), pltpu.VMEM((2, HB, C, LANES), F32),
                        pltpu.VMEM((2, HB, SUBLANES, LANES), F32)],
        compiler_params=_cparams(("arbitrary", "arbitrary")),
    )(log_gamma, qb, kb, vb, qb, kb, vb)


def _attn_kernel(lam_ref, q_ref, k_ref, v_ref, o_ref, qm_scr, m_scr, l_scr, acc_scr, *, ctx, tq, tk, nk):
    qi = pl.program_id(2)
    ki = pl.program_id(3)

    @pl.when(ki == 0)
    def _():
        q = q_ref[0]
        lane = lax.broadcasted_iota(jnp.int32, q.shape, 1)
        zero = jnp.zeros_like(q)
        qm_scr[0] = jnp.where(lane < DHC, q, zero)
        qm_scr[1] = jnp.where(lane < DHC, zero, q)
        m_scr[...] = jnp.full_like(m_scr, NEG_BIG)
        l_scr[...] = jnp.zeros_like(l_scr)
        acc_scr[...] = jnp.zeros_like(acc_scr)

    def step(masked):
        k = k_ref[0]
        v = v_ref[0]
        if masked:
            rows = lax.broadcasted_iota(jnp.int32, (tq, tk), 0)
            cols = lax.broadcasted_iota(jnp.int32, (tq, tk), 1) + ki * tk
            hidden = (rows < ctx) & (cols >= ctx)
        for c in range(2):
            s = _dot_nt(qm_scr[c], k)
            if masked:
                s = jnp.where(hidden, NEG_BIG, s)
            m_prev = m_scr[c]
            m_new = jnp.maximum(m_prev, jnp.max(s, axis=-1, keepdims=True))
            alpha = jnp.exp(m_prev - m_new)
            p = jnp.exp(s - m_new)
            l_scr[c] = alpha * l_scr[c] + jnp.sum(p, axis=-1, keepdims=True)
            acc_scr[c] = alpha * acc_scr[c] + _dot(p.astype(BF16), v)
            m_scr[c] = m_new

    @pl.when(qi == 0)
    def _():
        step(True)

    @pl.when(qi > 0)
    def _():
        step(False)

    @pl.when(ki == nk - 1)
    def _():
        o_ref[0] = acc_scr[0] / l_scr[0] - lam_ref[0] * (acc_scr[1] / l_scr[1])


def _diff_attention(qd, kd, vd, lam, ctx):
    B, TT, _ = qd.shape
    tq = tk = _pick(TT, (768, 384, 256, 128))
    assert ctx <= tk and ctx <= tq
    nq, nk = TT // tq, TT // tk
    return pl.pallas_call(
        functools.partial(_attn_kernel, ctx=ctx, tq=tq, tk=tk, nk=nk),
        grid=(B, HC, nq, nk),
        in_specs=[pl.BlockSpec(memory_space=pltpu.SMEM),
                  pl.BlockSpec((1, tq, LANES), lambda b, h, i, j: (b, i, h)),
                  pl.BlockSpec((1, tk, LANES), lambda b, h, i, j: (b, j, h)),
                  pl.BlockSpec((1, tk, LANES), lambda b, h, i, j: (b, j, h))],
        out_specs=pl.BlockSpec((1, tq, LANES), lambda b, h, i, j: (b, i, h)),
        out_shape=jax.ShapeDtypeStruct((B, TT, HC * DVC), F32),
        scratch_shapes=[pltpu.VMEM((2, tq, LANES), BF16), pltpu.VMEM((2, tq, 1), F32),
                        pltpu.VMEM((2, tq, 1), F32), pltpu.VMEM((2, tq, DVC), F32)],
        compiler_params=_cparams(("arbitrary", "arbitrary", "arbitrary", "arbitrary")),
    )(lam, qd, kd, vd)


def _merge_kernel(af_ref, ab_ref, bf_ref, bb_ref, c_ref, ga_ref, gb_ref, na_ref, nb_ref, nc_ref, o_ref, *, c_scale):
    oa = af_ref[0] + ab_ref[0]
    ob = bf_ref[0] + bb_ref[0]
    oc = c_ref[0]
    ga, gb = ga_ref[0], gb_ref[0]
    for h in range(HA):
        hs = slice(h * LANES, (h + 1) * LANES)
        o_ref[0, :, hs] = (_rms(oa[:, hs]) * na_ref[:, hs] * _silu(ga[:, hs])).astype(o_ref.dtype)
    for h in range(HB):
        hs = slice(h * LANES, (h + 1) * LANES)
        os_ = slice(WA + h * LANES, WA + (h + 1) * LANES)
        o_ref[0, :, os_] = (_rms(ob[:, hs]) * nb_ref[:, hs] * _silu(gb[:, hs])).astype(o_ref.dtype)
    for h in range(HC):
        hs = slice(h * LANES, (h + 1) * LANES)
        os_ = slice(2 * WA + h * LANES, 2 * WA + (h + 1) * LANES)
        o_ref[0, :, os_] = (_rms(oc[:, hs]) * nc_ref[:, hs] * c_scale).astype(o_ref.dtype)


def _merge(oa_f, oa_b, ob_f, ob_b, oc, p, g_a, g_b, g_c, lam_init, ctx):
    B, TT, _ = oc.shape
    bt = _pick(ctx, (256, 128, 64))
    s512 = pl.BlockSpec((1, bt, WA), lambda b, j: (b, j, 0))
    pcol = lambda c: pl.BlockSpec((1, bt, WA), lambda b, j: (b, j, c))
    return pl.pallas_call(
        functools.partial(_merge_kernel, c_scale=1.0 - lam_init),
        grid=(B, TT // bt),
        in_specs=[s512, s512, s512, s512, pl.BlockSpec((1, bt, 2 * WA), lambda b, j: (b, j, 0)),
                  pcol(COL_GA), pcol(COL_GB),
                  pl.BlockSpec((1, WA), lambda b, j: (0, 0)), pl.BlockSpec((1, WA), lambda b, j: (0, 0)),
                  pl.BlockSpec((1, 2 * WA), lambda b, j: (0, 0))],
        out_specs=pl.BlockSpec((1, bt, D_MIX), lambda b, j: (b, j, 0)),
        out_shape=jax.ShapeDtypeStruct((B, TT, D_MIX), BF16),
        compiler_params=_cparams(("arbitrary", "arbitrary")),
    )(oa_f, oa_b, ob_f, ob_b, oc, p, p, g_a.reshape(1, -1), g_b.reshape(1, -1), g_c.reshape(1, -1))


def kernel(x, c, ctx, c_ctx, w_ada, b_ada, norm_mix, w_in, hgrn_lower_bounds, ret_decay, diff_lambda, norm_a, norm_b, norm_c, w_out, norm_ffn, ffn_w1, ffn_w3, ffn_w2, router, moe_w1, moe_w3, moe_w2, final_norm):
    B, S, D = x.shape
    CTX = ctx.shape[1]
    L = w_ada.shape[0]
    TT = CTX + S
    M = B * TT
    F = ffn_w1.shape[-1]
    assert w_in.shape[-1] == D_PROJ and w_out.shape[1] == D_MIX
    assert B + 1 <= SUBLANES and CTX % CHUNK_B == 0 and S % CHUNK_B == 0 and S % GRID_W == 0

    lbs = jnp.cumsum(jax.nn.softmax(hgrn_lower_bounds.astype(F32), axis=0), axis=0)
    lbs = lbs - lbs[0:1]
    lb_tabs = jnp.stack([jnp.log(lbs), jnp.log1p(-lbs)], axis=2)
    log_gammas = jnp.log1p(-jnp.exp2(-ret_decay.astype(F32)))
    tabs = _rope_tables(S, CTX, DKB) + _rope_tables(S, CTX, DHC)
    tabs = tabs[:2] + tuple(jnp.concatenate([t, t], axis=-1) for t in tabs[2:])

    cvec = jnp.zeros((SUBLANES, D), F32).at[:B].set(c).at[B].set(c_ctx)
    mods = _ada_all(cvec, w_ada, b_ada)
    mods = mods.reshape(L, SUBLANES, 6, D).transpose(0, 2, 1, 3).reshape(L * 6 * SUBLANES, 1, D)
    mod_base = lambda l, chunk: (l * 6 + chunk) * SUBLANES

    bm = _pick(TT, (768, 384, 256, 128))
    bn_in = _pick(D_PROJ, (1280, 768, 512))
    bn_out = _pick(D, (1024, 512, 256))
    bf = _pick(F, (256, 128))

    xc = jnp.concatenate([ctx, x], axis=1)
    for l in range(L):
        lam_init = 0.8 - 0.6 * math.exp(-0.3 * l)
        lq1, lk1, lq2, lk2 = diff_lambda[l].astype(F32)
        lam = (jnp.exp(jnp.sum(lq1 * lk1)) - jnp.exp(jnp.sum(lq2 * lk2)) + lam_init).reshape(1)

        h = _modulate(xc, norm_mix[l], mods, mod_base(l, 0), mod_base(l, 1), CTX)
        p = _matmul(h.reshape(M, D), w_in[l].astype(BF16), F32, bm, bn_in).reshape(B, TT, D_PROJ)
        oa_f, oa_b = _hgrn(p, lb_tabs[l], CTX)
        qb, kb, vb, qd, kd, vd = _prep(p, tabs, CTX)
        ob_f, ob_b = _retention(qb, kb, vb, log_gammas[l], CTX)
        oc = _diff_attention(qd, kd, vd, lam, CTX)
        y = _merge(oa_f, oa_b, ob_f, ob_b, oc, p, norm_a[l], norm_b[l], norm_c[l], lam_init, CTX)
        xc = _matmul_residual(y.reshape(M, D_MIX), w_out[l].astype(BF16), xc.reshape(M, D), mods,
                              mod_base(l, 2), B, CTX, bm, bn_out).reshape(B, TT, D)

        if l % 2 == 0:
            h = _modulate(xc, norm_ffn[l], mods, mod_base(l, 3), mod_base(l, 4), CTX)
            e = l // 2
            xc = _ffn(h.reshape(M, D), ffn_w1[e].astype(BF16), ffn_w3[e].astype(BF16), ffn_w2[e].astype(BF16),
                      xc.reshape(M, D), mods, mod_base(l, 5), B, CTX, bm, bf).reshape(B, TT, D)
        else:
            e = l // 2
            wr = jnp.zeros((D, LANES), F32).at[:, :N_EXPERTS].set(router[e])
            wr_hi = wr.astype(BF16)
            wr2 = jnp.stack([wr_hi, (wr - wr_hi.astype(F32)).astype(BF16)])
            h, gates = _modulate(xc, norm_ffn[l], mods, mod_base(l, 3), mod_base(l, 4), CTX, router=wr2)
            xc = _moe_dense(h.reshape(M, D), gates.reshape(M, LANES), moe_w1[e].astype(BF16), moe_w3[e].astype(BF16),
                            moe_w2[e].astype(BF16), xc.reshape(M, D), mods, mod_base(l, 5), B, CTX, bm, bf).reshape(B, TT, D)

    zeros = jnp.zeros((SUBLANES, 1, D), F32)
    return _modulate(xc, final_norm, zeros, 0, 0, 0, out_dtype=F32, row_offset=CTX, rows=S)
```

```python
import functools
import math

import numpy as np
import jax
import jax.numpy as jnp
from jax import lax
from jax.experimental import pallas as pl
from jax.experimental.pallas import tpu as pltpu

F32 = jnp.float32
BF16 = jnp.bfloat16

HA, DKA, DVA = 4, 128, 128
HB, DKB, DVB = 4, 128, 128
HC, DHC, DVC = 8, 64, 128
GRID_W = 64
N_EXPERTS = 8
ROPE_BASE = 10000.0
EPS = 1e-6
WA = HA * DKA
D_PROJ = 5 * WA + 4 * WA + 3 * HC * DVC
D_MIX = HA * DVA + HB * DVB + HC * DVC
COL_QA, COL_FF, COL_FB, COL_IA, COL_GA, COL_QB, COL_KB, COL_VB, COL_GB = range(9)
COL_QD, COL_KD, COL_VD = 9, 11, 13

LANES = 128
SUBLANES = 8
VMEM_LIMIT_MB = 56

CHUNK_A = 64
LEVELS_A = (32, 16, 8)
CHUNK_B = 256
NEG_BIG = -1e30
Q_SCALE_C = DHC ** -0.5 * math.log2(math.e)
Q_GROUP = 256
K_SUB = 256
SCORE_LOOKAHEAD = 2


def _cparams(semantics, vmem_mb=VMEM_LIMIT_MB):
    return pltpu.CompilerParams(dimension_semantics=semantics, vmem_limit_bytes=vmem_mb * 2 ** 20)


def _pick(n, candidates):
    for c in candidates:
        if n % c == 0:
            return c
    raise ValueError(f"no block size for {n} in {candidates}")


def _split3(x):
    hi = x.astype(BF16)
    r1 = x - hi.astype(F32)
    mid = r1.astype(BF16)
    lo = (r1 - mid.astype(F32)).astype(BF16)
    return hi, mid, lo


def _dot(a, b):
    return jnp.dot(a, b, preferred_element_type=F32)


def _dot_nt(a, b):
    return lax.dot_general(a, b, (((1,), (1,)), ((), ())), preferred_element_type=F32)


def _dot_tn(a, b):
    return lax.dot_general(a, b, (((0,), (0,)), ((), ())), preferred_element_type=F32)


def _ada_kernel(c_ref, w_ref, b_ref, o_ref):
    c = c_ref[...]
    s = c * jax.nn.sigmoid(c)
    o_ref[0] = _dot(s.astype(BF16), w_ref[0].astype(BF16)) + b_ref[0]


def _ada_all(cvec, w_ada, b_ada):
    L, D, N = w_ada.shape
    bn = _pick(N, (1024, 768, 512, 256, 128))
    return pl.pallas_call(
        _ada_kernel,
        grid=(L, N // bn),
        in_specs=[
            pl.BlockSpec((SUBLANES, D), lambda l, j: (0, 0)),
            pl.BlockSpec((1, D, bn), lambda l, j: (l, 0, j)),
            pl.BlockSpec((1, 1, bn), lambda l, j: (l, 0, j)),
        ],
        out_specs=pl.BlockSpec((1, SUBLANES, bn), lambda l, j: (l, 0, j)),
        out_shape=jax.ShapeDtypeStruct((L, SUBLANES, N), F32),
        compiler_params=_cparams(("arbitrary", "arbitrary")),
    )(cvec, w_ada, b_ada.reshape(L, 1, N))


def _rms(x):
    return x * lax.rsqrt(jnp.mean(x * x, axis=-1, keepdims=True) + EPS)


def _modulate_kernel(x_ref, g_ref, sh_ref, sc_ref, o_ref):
    y = _rms(x_ref[0]) * g_ref[...]
    o_ref[0] = (y * (1 + sc_ref[0]) + sh_ref[0]).astype(o_ref.dtype)


def _modulate_route_kernel(x_ref, g_ref, sh_ref, sc_ref, wr_ref, o_ref, gate_ref):
    y = _rms(x_ref[0]) * g_ref[...]
    h = y * (1 + sc_ref[0]) + sh_ref[0]
    o_ref[0] = h.astype(o_ref.dtype)
    hi = h.astype(BF16)
    lo = (h - hi.astype(F32)).astype(BF16)
    logits = _dot(hi, wr_ref[0]) + (_dot(lo, wr_ref[0]) + _dot(hi, wr_ref[1]))
    lane = lax.broadcasted_iota(jnp.int32, logits.shape, 1)
    logits = jnp.where(lane < N_EXPERTS, logits, -jnp.inf)
    m1 = jnp.max(logits, axis=-1, keepdims=True)
    i1 = jnp.min(jnp.where(logits == m1, lane, LANES), axis=-1, keepdims=True)
    rest = jnp.where(lane == i1, -jnp.inf, logits)
    m2 = jnp.max(rest, axis=-1, keepdims=True)
    i2 = jnp.min(jnp.where(rest == m2, lane, LANES), axis=-1, keepdims=True)
    e2 = jnp.exp(m2 - m1)
    w1 = 1.0 / (1.0 + e2)
    gate_ref[0] = jnp.where(lane == i1, w1, 0.0) + jnp.where(lane == i2, e2 * w1, 0.0)


def _mod_index(base, nctx_blocks):
    def index(b, j, *, nb):
        return (base + jnp.where(j < nctx_blocks, nb, b), 0, 0)
    return index


def _modulate(xc, g, mods, base_shift, base_scale, ctx, out_dtype=BF16, router=None, row_offset=0, rows=None):
    B, TT, D = xc.shape
    bt = _pick(ctx, (256, 128, 64))
    rows = TT if rows is None else rows
    off = row_offset // bt
    nctx = max(ctx // bt - off, 0)
    sh_idx = functools.partial(_mod_index(base_shift, nctx), nb=B)
    sc_idx = functools.partial(_mod_index(base_scale, nctx), nb=B)
    in_specs = [
        pl.BlockSpec((1, bt, D), lambda b, j: (b, j + off, 0)),
        pl.BlockSpec((1, D), lambda b, j: (0, 0)),
        pl.BlockSpec((1, 1, D), sh_idx),
        pl.BlockSpec((1, 1, D), sc_idx),
    ]
    out_spec = pl.BlockSpec((1, bt, D), lambda b, j: (b, j, 0))
    out_shape = jax.ShapeDtypeStruct((B, rows, D), out_dtype)
    if router is None:
        return pl.pallas_call(
            _modulate_kernel, grid=(B, rows // bt), in_specs=in_specs, out_specs=out_spec, out_shape=out_shape,
            compiler_params=_cparams(("arbitrary", "arbitrary")),
        )(xc, g.reshape(1, D), mods, mods)
    in_specs.append(pl.BlockSpec((2, D, LANES), lambda b, j: (0, 0, 0)))
    return pl.pallas_call(
        _modulate_route_kernel, grid=(B, rows // bt), in_specs=in_specs,
        out_specs=[out_spec, pl.BlockSpec((1, bt, LANES), lambda b, j: (b, j, 0))],
        out_shape=[out_shape, jax.ShapeDtypeStruct((B, rows, LANES), F32)],
        compiler_params=_cparams(("arbitrary", "arbitrary")),
    )(xc, g.reshape(1, D), mods, mods, router)


def _mm_kernel(a_ref, w_ref, o_ref):
    o_ref[...] = _dot(a_ref[...], w_ref[...]).astype(o_ref.dtype)


def _matmul(a, w, out_dtype, bm, bn):
    M, K = a.shape
    N = w.shape[1]
    return pl.pallas_call(
        _mm_kernel, grid=(M // bm, N // bn),
        in_specs=[pl.BlockSpec((bm, K), lambda i, j: (i, 0)), pl.BlockSpec((K, bn), lambda i, j: (0, j))],
        out_specs=pl.BlockSpec((bm, bn), lambda i, j: (i, j)),
        out_shape=jax.ShapeDtypeStruct((M, N), out_dtype),
        compiler_params=_cparams(("arbitrary", "arbitrary")),
    )(a, w)


def _row_gate(shape, i, nb_per_batch, ctx, g_ctx, g_lat):
    nctx = jnp.where(i % nb_per_batch == 0, ctx, 0)
    rows = lax.broadcasted_iota(jnp.int32, shape, 0)
    return jnp.where(rows < nctx, g_ctx, g_lat)


def _mm_res_kernel(a_ref, w_ref, x_ref, gl_ref, gc_ref, o_ref, *, nb_per_batch, ctx):
    acc = _dot(a_ref[...], w_ref[...])
    g = _row_gate(acc.shape, pl.program_id(0), nb_per_batch, ctx, gc_ref[0], gl_ref[0])
    o_ref[...] = x_ref[...] + g * acc


def _matmul_residual(a, w, x2, mods, base_gate, B, ctx, bm, bn):
    M, K = a.shape
    N = w.shape[1]
    nbb = (M // B) // bm
    return pl.pallas_call(
        functools.partial(_mm_res_kernel, nb_per_batch=nbb, ctx=ctx),
        grid=(M // bm, N // bn),
        in_specs=[
            pl.BlockSpec((bm, K), lambda i, j: (i, 0)),
            pl.BlockSpec((K, bn), lambda i, j: (0, j)),
            pl.BlockSpec((bm, bn), lambda i, j: (i, j)),
            pl.BlockSpec((1, 1, bn), lambda i, j: (base_gate + i // nbb, 0, j)),
            pl.BlockSpec((1, 1, bn), lambda i, j: (base_gate + B, 0, j)),
        ],
        out_specs=pl.BlockSpec((bm, bn), lambda i, j: (i, j)),
        out_shape=jax.ShapeDtypeStruct((M, N), F32),
        compiler_params=_cparams(("arbitrary", "arbitrary")),
    )(a, w, x2, mods, mods)


def _silu(x):
    return x * jax.nn.sigmoid(x)


def _ffn_kernel(h_ref, w1_ref, w3_ref, w2_ref, x_ref, gl_ref, gc_ref, o_ref, *, nb_per_batch, ctx, nf):
    f = pl.program_id(1)
    h = h_ref[...]
    hid = (_silu(_dot(h, w1_ref[...])) * _dot(h, w3_ref[...])).astype(BF16)
    part = _dot(hid, w2_ref[...])

    @pl.when(f == 0)
    def _():
        o_ref[...] = part

    @pl.when(f > 0)
    def _():
        o_ref[...] += part

    @pl.when(f == nf - 1)
    def _():
        g = _row_gate(o_ref.shape, pl.program_id(0), nb_per_batch, ctx, gc_ref[0], gl_ref[0])
        o_ref[...] = x_ref[...] + g * o_ref[...]


def _ffn(h, w1, w3, w2, x2, mods, base_gate, B, ctx, bm, bf):
    M, D = h.shape
    F = w1.shape[1]
    nf = F // bf
    nbb = (M // B) // bm
    return pl.pallas_call(
        functools.partial(_ffn_kernel, nb_per_batch=nbb, ctx=ctx, nf=nf),
        grid=(M // bm, nf),
        in_specs=[
            pl.BlockSpec((bm, D), lambda i, f: (i, 0)),
            pl.BlockSpec((D, bf), lambda i, f: (0, f)),
            pl.BlockSpec((D, bf), lambda i, f: (0, f)),
            pl.BlockSpec((bf, D), lambda i, f: (f, 0)),
            pl.BlockSpec((bm, D), lambda i, f: (i, 0)),
            pl.BlockSpec((1, 1, D), lambda i, f: (base_gate + i // nbb, 0, 0)),
            pl.BlockSpec((1, 1, D), lambda i, f: (base_gate + B, 0, 0)),
        ],
        out_specs=pl.BlockSpec((bm, D), lambda i, f: (i, 0)),
        out_shape=jax.ShapeDtypeStruct((M, D), F32),
        compiler_params=_cparams(("arbitrary", "arbitrary")),
    )(h, w1, w3, w2, x2, mods, mods)


def _moe_sparse_kernel(cnt_ref, h_ref, gate_ref, rank_ref, rankt_ref, w1_ref, w3_ref, w2_ref, o_ref,
                       hg_scr, acc_scr, *, nf, sub, first_slot):
    i, e, f = pl.program_id(0), pl.program_id(1), pl.program_id(2)
    active = cnt_ref[i * N_EXPERTS + e] > first_slot

    @pl.when((e == 0) & (f == 0))
    def _():
        o_ref[...] = jnp.zeros_like(o_ref)

    @pl.when(active & (f == 0))
    def _():
        slot = lax.broadcasted_iota(jnp.int32, (sub, h_ref.shape[0]), 0) + first_slot
        pack = jnp.where(slot == rankt_ref[0, 0], 1.0, 0.0).astype(BF16)
        hg_scr[...] = _dot(pack, h_ref[...]).astype(BF16)
        acc_scr[...] = jnp.zeros_like(acc_scr)

    @pl.when(active)
    def _():
        hg = hg_scr[...]
        hid = (_silu(_dot(hg, w1_ref[0])) * _dot(hg, w3_ref[0])).astype(BF16)
        acc_scr[...] += _dot(hid, w2_ref[0])

    @pl.when(active & (f == nf - 1))
    def _():
        lane = lax.broadcasted_iota(jnp.int32, gate_ref.shape, 1)
        ge = jnp.sum(jnp.where(lane == e, gate_ref[...], 0.0), axis=-1, keepdims=True)
        rank = jnp.sum(jnp.where(lane == e, rank_ref[...], 0), axis=-1, keepdims=True)
        slot = lax.broadcasted_iota(jnp.int32, (h_ref.shape[0], sub), 1) + first_slot
        unpack = jnp.where(slot == rank, 1.0, 0.0).astype(BF16)
        o_ref[...] += ge * _dot(unpack, acc_scr[...].astype(BF16))


def _moe_sparse(h, gates, w1, w3, w2, bm, bf):
    M, D = h.shape
    E, _, F = w1.shape
    nf, nblk = F // bf, M // bm
    sub = bm // 3
    routed = gates[:, :E].reshape(nblk, bm, E) > 0
    csum = jnp.cumsum(routed.astype(jnp.int32), axis=1)
    rank = jnp.where(routed, csum - 1, -1)
    counts = csum[:, -1, :].reshape(nblk * E)
    rank_lane = jnp.full((M, LANES), -1, jnp.int32).at[:, :E].set(rank.reshape(M, E))
    rank_t = rank.transpose(0, 2, 1).reshape(nblk, E, 1, bm)

    def run(first_slot):
        return pl.pallas_call(
            functools.partial(_moe_sparse_kernel, nf=nf, sub=sub, first_slot=first_slot),
            grid_spec=pltpu.PrefetchScalarGridSpec(
                num_scalar_prefetch=1,
                grid=(nblk, E, nf),
                in_specs=[
                    pl.BlockSpec((bm, D), lambda i, e, f, c: (i, 0)),
                    pl.BlockSpec((bm, LANES), lambda i, e, f, c: (i, 0)),
                    pl.BlockSpec((bm, LANES), lambda i, e, f, c: (i, 0)),
                    pl.BlockSpec((1, 1, 1, bm), lambda i, e, f, c: (i, e, 0, 0)),
                    pl.BlockSpec((1, D, bf), lambda i, e, f, c: (e, 0, f)),
                    pl.BlockSpec((1, D, bf), lambda i, e, f, c: (e, 0, f)),
                    pl.BlockSpec((1, bf, D), lambda i, e, f, c: (e, f, 0)),
                ],
                out_specs=pl.BlockSpec((bm, D), lambda i, e, f, c: (i, 0)),
                scratch_shapes=[pltpu.VMEM((sub, D), BF16), pltpu.VMEM((sub, D), F32)],
            ),
            out_shape=jax.ShapeDtypeStruct((M, D), F32),
            compiler_params=_cparams(("arbitrary", "arbitrary", "arbitrary")),
        )(counts, h, gates, rank_lane, rank_t, w1, w3, w2)

    delta = run(0)
    for first_slot in range(sub, bm, sub):
        delta = lax.cond(jnp.max(counts) > first_slot, lambda d, s=first_slot: d + run(s), lambda d: d, delta)
    return delta


def _residual_kernel(x_ref, d_ref, g_ref, o_ref):
    o_ref[0] = x_ref[0] + g_ref[0] * d_ref[0]


def _gated_residual(xc, delta, mods, base_gate, ctx):
    B, TT, D = xc.shape
    bt = _pick(ctx, (256, 128, 64))
    blk = pl.BlockSpec((1, bt, D), lambda b, j: (b, j, 0))
    return pl.pallas_call(
        _residual_kernel, grid=(B, TT // bt),
        in_specs=[blk, blk, pl.BlockSpec((1, 1, D), functools.partial(_mod_index(base_gate, ctx // bt), nb=B))],
        out_specs=blk, out_shape=jax.ShapeDtypeStruct((B, TT, D), F32),
        compiler_params=_cparams(("arbitrary", "arbitrary")),
    )(xc, delta, mods)


def _rope_tables(seq, ctx, d):
    nf = d // 4
    rows = seq // GRID_W
    row = jnp.broadcast_to(jnp.arange(rows, dtype=jnp.int32)[:, None], (rows, GRID_W)).reshape(seq)
    col = jnp.broadcast_to(jnp.arange(GRID_W, dtype=jnp.int32)[None, :], (rows, GRID_W)).reshape(seq)
    inv_freq = ROPE_BASE ** (-jnp.arange(nf, dtype=F32) / nf)
    ar = row.astype(F32)[:, None] * inv_freq
    ac = col.astype(F32)[:, None] * inv_freq
    cos = jnp.concatenate([jnp.cos(ar), jnp.cos(ar), jnp.cos(ac), jnp.cos(ac)], axis=-1)
    sin = jnp.concatenate([-jnp.sin(ar), jnp.sin(ar), -jnp.sin(ac), jnp.sin(ac)], axis=-1)
    cos = jnp.concatenate([jnp.ones((ctx, d), F32), cos], axis=0)
    sin = jnp.concatenate([jnp.zeros((ctx, d), F32), sin], axis=0)
    return cos, sin


def _rope(x, cos, sin, quarter):
    lane = lax.broadcasted_iota(jnp.int32, x.shape, 1)
    up = pltpu.roll(x, LANES - quarter, 1)
    down = pltpu.roll(x, quarter, 1)
    swapped = jnp.where(lane % (2 * quarter) < quarter, up, down)
    return x * cos + swapped * sin


def _prep_kernel(qb_ref, kb_ref, vb_ref, qd0_ref, qd1_ref, kd0_ref, kd1_ref, vd0_ref, vd1_ref,
                 cb_ref, sb_ref, cc_ref, sc_ref, oqb, okb, ovb, oqd, okd, ovd):
    cb, sb, cc, sc = cb_ref[...], sb_ref[...], cc_ref[...], sc_ref[...]
    qb, kb = qb_ref[0], kb_ref[0]
    for h in range(HB):
        hs = slice(h * LANES, (h + 1) * LANES)
        oqb[0, :, hs] = _rope(qb[:, hs], cb, sb, DKB // 4).astype(BF16)
        okb[0, :, hs] = (_rope(kb[:, hs], cb, sb, DKB // 4) * DKB ** -0.5).astype(BF16)
    ovb[0] = vb_ref[0].astype(BF16)
    for half, (qr, kr, vr) in enumerate(((qd0_ref, kd0_ref, vd0_ref), (qd1_ref, kd1_ref, vd1_ref))):
        q, k, v = qr[0], kr[0], vr[0]
        for h in range(WA // LANES):
            hs = slice(h * LANES, (h + 1) * LANES)
            head = half * (WA // LANES) + h
            oqd[0, head] = (_rope(q[:, hs], cc, sc, DHC // 4) * Q_SCALE_C).T.astype(BF16)
            okd[0, :, half * WA + h * LANES:half * WA + (h + 1) * LANES] = _rope(k[:, hs], cc, sc, DHC // 4).astype(BF16)
            ovd[0, head] = v[:, hs].T.astype(BF16)


def _prep(p, tabs, ctx):
    B, TT, _ = p.shape
    bt = _pick(ctx, (256, 128, 64))
    pspec = lambda c: pl.BlockSpec((1, bt, WA), lambda b, j: (b, j, c))
    tspec = pl.BlockSpec((bt, LANES), lambda b, j: (j, 0))
    o512 = pl.BlockSpec((1, bt, WA), lambda b, j: (b, j, 0))
    o1024 = pl.BlockSpec((1, bt, 2 * WA), lambda b, j: (b, j, 0))
    o_t = pl.BlockSpec((1, HC, LANES, bt), lambda b, j: (b, 0, 0, j))
    s512 = jax.ShapeDtypeStruct((B, TT, WA), BF16)
    s1024 = jax.ShapeDtypeStruct((B, TT, 2 * WA), BF16)
    s_t = jax.ShapeDtypeStruct((B, HC, LANES, TT), BF16)
    cols = (COL_QB, COL_KB, COL_VB, COL_QD, COL_QD + 1, COL_KD, COL_KD + 1, COL_VD, COL_VD + 1)
    return pl.pallas_call(
        _prep_kernel, grid=(B, TT // bt),
        in_specs=[pspec(c) for c in cols] + [tspec] * 4,
        out_specs=[o512, o512, o512, o_t, o1024, o_t],
        out_shape=[s512, s512, s512, s_t, s1024, s_t],
        compiler_params=_cparams(("arbitrary", "arbitrary")),
    )(*([p] * 9), *tabs)


def _hgrn_consts():
    C = CHUNK_A
    t = np.arange(C)[:, None]
    u = np.arange(C)[None, :]
    out = []
    for rev in (False, True):
        mats = [(u >= t) if rev else (u <= t)]
        for m in LEVELS_A:
            r = (t // (2 * m)) * (2 * m) + (m if rev else m - 1)
            mats.append((u >= r) if rev else (u <= r))
        out.append(np.concatenate(mats, axis=0))
    return jnp.asarray(np.stack(out).astype(np.float32), dtype=BF16)


def _hgrn_direction(rev, q_ref, z_ref, v_ref, o_ref, cm, llb, l1m, s_scr, k_scr, b_scr, d_scr):
    C = CHUNK_A
    z = z_ref[0]
    log_sig = jnp.minimum(z, 0.0) - jnp.log1p(jnp.exp(-jnp.abs(z)))
    t2 = l1m + log_sig
    logf = jnp.maximum(llb, t2) + jnp.log1p(jnp.exp(-jnp.abs(llb - t2)))
    k_scr[...] = 1.0 - jnp.exp(logf)
    parts = _split3(logf)
    bb = _dot(cm, parts[0]) + (_dot(cm, parts[1]) + _dot(cm, parts[2]))
    b_scr[...] = bb[:C]
    ones = jnp.ones((C, LANES), BF16)
    bend_t = _dot_tn(parts[0], ones) + (_dot_tn(parts[1], ones) + _dot_tn(parts[2], ones))

    row = lax.broadcasted_iota(jnp.int32, (C, LANES), 0)
    rr = lax.broadcasted_iota(jnp.int32, (C, C), 0)
    cc = lax.broadcasted_iota(jnp.int32, (C, C), 1)
    sub_row = lax.broadcasted_iota(jnp.int32, (SUBLANES, LANES), 0)
    end = 0 if rev else C - 1

    def diag_block(blk, carry):
        r0 = pl.multiple_of(blk * SUBLANES, SUBLANES)
        for h in range(HA):
            hs = slice(h * LANES, (h + 1) * LANES)
            qi = q_ref[0, pl.ds(r0, SUBLANES), hs]
            bi = b_scr[pl.ds(r0, SUBLANES), hs]
            ki = k_scr[pl.ds(r0, SUBLANES), hs]
            vi = v_ref[0, pl.ds(r0, SUBLANES), hs]
            acc = jnp.zeros((SUBLANES, LANES), F32)
            for j in range(SUBLANES):
                kj, bj, vj = ki[j:j + 1, :], bi[j:j + 1, :], vi[j:j + 1, :]
                e = jnp.exp(jnp.minimum(bi - bj, 0.0))
                a = jnp.sum(qi * kj * e, axis=-1, keepdims=True)
                valid = (sub_row <= j) if rev else (sub_row >= j)
                acc = acc + jnp.where(valid, a, 0.0) * vj
            d_scr[pl.ds(r0, SUBLANES), hs] = acc
        return carry

    lax.fori_loop(0, C // SUBLANES, diag_block, 0)

    for h in range(HA):
        hs = slice(h * LANES, (h + 1) * LANES)
        q = q_ref[0, :, hs]
        k = k_scr[:, hs]
        v = v_ref[0, :, hs].astype(BF16)
        b = bb[:C, hs]
        s_old = s_scr[h]
        inter = _dot((q * jnp.exp(b)).astype(BF16), s_old.astype(BF16))
        att = jnp.zeros((C, C), F32)
        for lvl, m in enumerate(LEVELS_A):
            bref = bb[(lvl + 1) * C:(lvl + 2) * C, hs]
            e = jnp.exp(-jnp.abs(b - bref))
            upper = (row % (2 * m)) >= m
            q_side = jnp.logical_not(upper) if rev else upper
            qt = jnp.where(q_side, q * e, 0.0).astype(BF16)
            kt = jnp.where(q_side, 0.0, k * e).astype(BF16)
            a = _dot_nt(qt, kt)
            att = att + jnp.where((rr // (2 * m)) == (cc // (2 * m)), a, 0.0)
        o_ref[0, :, hs] = inter + _dot(att.astype(BF16), v) + d_scr[:, hs]
        bend = b[end:end + 1, :]
        kdec = (k * jnp.exp(bend - b)).astype(BF16)
        s_scr[h] = jnp.exp(bend_t[hs, :]) * s_old + _dot_tn(kdec, v)


def _hgrn_kernel(qf_ref, zf_ref, vf_ref, qb_ref, zb_ref, vb_ref, cm_ref, lb_ref, of_ref, ob_ref,
                 s_scr, k_scr, b_scr, d_scr):
    @pl.when(pl.program_id(1) == 0)
    def _():
        s_scr[...] = jnp.zeros_like(s_scr)

    _hgrn_direction(False, qf_ref, zf_ref, vf_ref, of_ref, cm_ref[0], lb_ref[0, 0:1], lb_ref[0, 1:2],
                    s_scr.at[0], k_scr, b_scr, d_scr)
    _hgrn_direction(True, qb_ref, zb_ref, vb_ref, ob_ref, cm_ref[1], lb_ref[1, 0:1], lb_ref[1, 1:2],
                    s_scr.at[1], k_scr, b_scr, d_scr)


def _bwd_chunk(i, nctx, n):
    return jnp.where(i < nctx, nctx - 1 - i, n - 1 + nctx - i)


def _hgrn(p, lb_tab, ctx):
    B, TT, _ = p.shape
    C = CHUNK_A
    n, nctx = TT // C, ctx // C
    fwd = lambda c: pl.BlockSpec((1, C, WA), lambda b, i: (b, i, c))
    bwd = lambda c: pl.BlockSpec((1, C, WA), lambda b, i: (b, _bwd_chunk(i, nctx, n), c))
    cm = _hgrn_consts()
    shape = jax.ShapeDtypeStruct((B, TT, WA), F32)
    return pl.pallas_call(
        _hgrn_kernel, grid=(B, n),
        in_specs=[fwd(COL_QA), fwd(COL_FF), fwd(COL_IA), bwd(COL_QA), bwd(COL_FB), bwd(COL_IA),
                  pl.BlockSpec(cm.shape, lambda b, i: (0, 0, 0)),
                  pl.BlockSpec((2, 2, WA), lambda b, i: (0, 0, 0))],
        out_specs=[pl.BlockSpec((1, C, WA), lambda b, i: (b, i, 0)),
                   pl.BlockSpec((1, C, WA), lambda b, i: (b, _bwd_chunk(i, nctx, n), 0))],
        out_shape=[shape, shape],
        scratch_shapes=[pltpu.VMEM((2, HA, DKA, DVA), F32), pltpu.VMEM((C, WA), F32),
                        pltpu.VMEM((C, WA), F32), pltpu.VMEM((C, WA), F32)],
        compiler_params=_cparams(("arbitrary", "arbitrary")),
    )(p, p, p, p, p, p, cm, lb_tab)


def _ret_kernel(lg_ref, qf_ref, kf_ref, vf_ref, qb_ref, kb_ref, vb_ref, of_ref, ob_ref,
                s_scr, d_scr, rq_scr, rk_scr, gc_scr):
    C = CHUNK_B

    @pl.when((pl.program_id(0) == 0) & (pl.program_id(1) == 0))
    def _():
        t = lax.broadcasted_iota(jnp.int32, (C, C), 0)
        s = lax.broadcasted_iota(jnp.int32, (C, C), 1)
        pos = lax.broadcasted_iota(jnp.int32, (C, LANES), 0).astype(F32)
        for d in range(2):
            for h in range(HB):
                lg = lg_ref[d, h]
                delta = (s - t) if d else (t - s)
                d_scr[d, h] = jnp.where(delta >= 0, jnp.exp(jnp.maximum(delta, 0).astype(F32) * lg), 0.0)
                rq_scr[d, h] = jnp.exp(((C - pos) if d else (pos + 1.0)) * lg)
                rk_scr[d, h] = jnp.exp((pos if d else (C - 1.0 - pos)) * lg)
                gc_scr[d, h] = jnp.exp(jnp.full((SUBLANES, LANES), C, F32) * lg)

    @pl.when(pl.program_id(1) == 0)
    def _():
        s_scr[...] = jnp.zeros_like(s_scr)

    for d, (q_ref, k_ref, v_ref, o_ref) in enumerate(((qf_ref, kf_ref, vf_ref, of_ref), (qb_ref, kb_ref, vb_ref, ob_ref))):
        for h in range(HB):
            hs = slice(h * LANES, (h + 1) * LANES)
            q, k, v = q_ref[0, :, hs], k_ref[0, :, hs], v_ref[0, :, hs]
            att = (_dot_nt(q, k) * d_scr[d, h]).astype(BF16)
            s_old = s_scr[d, h]
            o_ref[0, :, hs] = _dot(att, v) + rq_scr[d, h] * _dot(q, s_old.astype(BF16))
            kdec = (k.astype(F32) * rk_scr[d, h]).astype(BF16)
            s_scr[d, h] = gc_scr[d, h][0:1, :] * s_old + _dot_tn(kdec, v)


def _retention(qb, kb, vb, log_gamma, ctx):
    B, TT, _ = qb.shape
    C = CHUNK_B
    n, nctx = TT // C, ctx // C
    fwd = pl.BlockSpec((1, C, WA), lambda b, i: (b, i, 0))
    bwd = pl.BlockSpec((1, C, WA), lambda b, i: (b, _bwd_chunk(i, nctx, n), 0))
    shape = jax.ShapeDtypeStruct((B, TT, WA), F32)
    return pl.pallas_call(
        _ret_kernel, grid=(B, n),
        in_specs=[pl.BlockSpec(memory_space=pltpu.SMEM), fwd, fwd, fwd, bwd, bwd, bwd],
        out_specs=[fwd, bwd], out_shape=[shape, shape],
        scratch_shapes=[pltpu.VMEM((2, HB, DKB, DVB), F32), pltpu.VMEM((2, HB, C, C), F32),
                        pltpu.VMEM((2, HB, C, LANES), F32), pltpu.VMEM((2, HB, C, LANES), F32),
                        pltpu.VMEM((2, HB, SUBLANES, LANES), F32)],
        compiler_params=_cparams(("arbitrary", "arbitrary")),
    )(log_gamma, qb, kb, vb, qb, kb, vb)


def _attn_kernel(lam_ref, q_ref, k_ref, v_ref, o_ref, qm_scr, m_scr, l_scr, acc_scr, *, ctx, tq, tk, nk):
    qi = pl.program_id(2)
    ki = pl.program_id(3)

    @pl.when(ki == 0)
    def _():
        q = q_ref[0, 0]
        row = lax.broadcasted_iota(jnp.int32, q.shape, 0)
        zero = jnp.zeros_like(q)
        qm_scr[0] = jnp.where(row < DHC, q, zero)
        qm_scr[1] = jnp.where(row < DHC, zero, q)
        m_scr[...] = jnp.full_like(m_scr, NEG_BIG)
        l_scr[...] = jnp.zeros_like(l_scr)
        acc_scr[...] = jnp.zeros_like(acc_scr)

    def step(masked):
        k = k_ref[0]
        vt = v_ref[0, 0]
        m_old, l_old, acc_old = m_scr[...], l_scr[...], acc_scr[...]
        key_subs = [slice(j, j + K_SUB) for j in range(0, tk, K_SUB)]
        m_out, l_out, acc_out = [[], []], [[], []], [[], []]
        chains = [(g, c) for g in range(tq // Q_GROUP) for c in range(2)]

        def scores(g, c):
            qg = qm_scr[c, :, g * Q_GROUP:(g + 1) * Q_GROUP]
            s = [_dot(k[ks], qg) for ks in key_subs]
            if masked and g * Q_GROUP < ctx:
                keys = lax.broadcasted_iota(jnp.int32, (tk, Q_GROUP), 0) + ki * tk
                qpos = lax.broadcasted_iota(jnp.int32, (tk, Q_GROUP), 1) + g * Q_GROUP
                hidden = (qpos < ctx) & (keys >= ctx)
                s = [jnp.where(hidden[ks], NEG_BIG, sj) for ks, sj in zip(key_subs, s)]
            return s

        def finish(g, c, s):
            gs = slice(g * Q_GROUP, (g + 1) * Q_GROUP)
            m_prev = m_old[c, :, gs]
            m_new = functools.reduce(jnp.maximum, [jnp.max(sj, axis=0, keepdims=True) for sj in s], m_prev)
            alpha = jnp.exp2(m_prev - m_new)
            p = [jnp.exp2(sj - m_new) for sj in s]
            l_new = functools.reduce(jnp.add, [jnp.sum(pj, axis=0, keepdims=True) for pj in p])
            pv = functools.reduce(jnp.add, [_dot(vt[:, ks], pj.astype(BF16)) for ks, pj in zip(key_subs, p)])
            l_out[c].append(alpha * l_old[c, :, gs] + l_new)
            acc_out[c].append(alpha * acc_old[c, :, gs] + pv)
            m_out[c].append(m_new)

        pending = [scores(*ch) for ch in chains[:SCORE_LOOKAHEAD]]
        for n, chain in enumerate(chains):
            if n + SCORE_LOOKAHEAD < len(chains):
                pending.append(scores(*chains[n + SCORE_LOOKAHEAD]))
            finish(*chain, pending.pop(0))
        for c in range(2):
            m_scr[c] = jnp.concatenate(m_out[c], axis=-1)
            l_scr[c] = jnp.concatenate(l_out[c], axis=-1)
            acc_scr[c] = jnp.concatenate(acc_out[c], axis=-1)

    @pl.when(qi == 0)
    def _():
        step(True)

    @pl.when(qi > 0)
    def _():
        step(False)

    @pl.when(ki == nk - 1)
    def _():
        o_t = acc_scr[0] / l_scr[0] - lam_ref[0] * (acc_scr[1] / l_scr[1])
        o_ref[0] = o_t.T


def _diff_attention(qd, kd, vd, lam, ctx):
    B, TT, _ = kd.shape
    tq = tk = _pick(TT, (768, 512, 256))
    assert ctx <= tk and ctx <= tq and tq % Q_GROUP == 0
    nq, nk = TT // tq, TT // tk
    return pl.pallas_call(
        functools.partial(_attn_kernel, ctx=ctx, tq=tq, tk=tk, nk=nk),
        grid=(B, HC, nq, nk),
        in_specs=[pl.BlockSpec(memory_space=pltpu.SMEM),
                  pl.BlockSpec((1, 1, LANES, tq), lambda b, h, i, j: (b, h, 0, i)),
                  pl.BlockSpec((1, tk, LANES), lambda b, h, i, j: (b, j, h)),
                  pl.BlockSpec((1, 1, LANES, tk), lambda b, h, i, j: (b, h, 0, j))],
        out_specs=pl.BlockSpec((1, tq, LANES), lambda b, h, i, j: (b, i, h)),
        out_shape=jax.ShapeDtypeStruct((B, TT, HC * DVC), F32),
        scratch_shapes=[pltpu.VMEM((2, LANES, tq), BF16), pltpu.VMEM((2, 1, tq), F32),
                        pltpu.VMEM((2, 1, tq), F32), pltpu.VMEM((2, DVC, tq), F32)],
        compiler_params=_cparams(("arbitrary", "arbitrary", "arbitrary", "arbitrary")),
    )(lam, qd, kd, vd)


def _merge_kernel(af_ref, ab_ref, bf_ref, bb_ref, c_ref, ga_ref, gb_ref, na_ref, nb_ref, nc_ref, o_ref, *, c_scale):
    oa = af_ref[0] + ab_ref[0]
    ob = bf_ref[0] + bb_ref[0]
    oc = c_ref[0]
    ga, gb = ga_ref[0], gb_ref[0]
    for h in range(HA):
        hs = slice(h * LANES, (h + 1) * LANES)
        o_ref[0, :, hs] = (_rms(oa[:, hs]) * na_ref[:, hs] * _silu(ga[:, hs])).astype(o_ref.dtype)
    for h in range(HB):
        hs = slice(h * LANES, (h + 1) * LANES)
        os_ = slice(WA + h * LANES, WA + (h + 1) * LANES)
        o_ref[0, :, os_] = (_rms(ob[:, hs]) * nb_ref[:, hs] * _silu(gb[:, hs])).astype(o_ref.dtype)
    for h in range(HC):
        hs = slice(h * LANES, (h + 1) * LANES)
        os_ = slice(2 * WA + h * LANES, 2 * WA + (h + 1) * LANES)
        o_ref[0, :, os_] = (_rms(oc[:, hs]) * nc_ref[:, hs] * c_scale).astype(o_ref.dtype)


def _merge(oa_f, oa_b, ob_f, ob_b, oc, p, g_a, g_b, g_c, lam_init, ctx):
    B, TT, _ = oc.shape
    bt = _pick(ctx, (256, 128, 64))
    s512 = pl.BlockSpec((1, bt, WA), lambda b, j: (b, j, 0))
    pcol = lambda c: pl.BlockSpec((1, bt, WA), lambda b, j: (b, j, c))
    return pl.pallas_call(
        functools.partial(_merge_kernel, c_scale=1.0 - lam_init),
        grid=(B, TT // bt),
        in_specs=[s512, s512, s512, s512, pl.BlockSpec((1, bt, 2 * WA), lambda b, j: (b, j, 0)),
                  pcol(COL_GA), pcol(COL_GB),
                  pl.BlockSpec((1, WA), lambda b, j: (0, 0)), pl.BlockSpec((1, WA), lambda b, j: (0, 0)),
                  pl.BlockSpec((1, 2 * WA), lambda b, j: (0, 0))],
        out_specs=pl.BlockSpec((1, bt, D_MIX), lambda b, j: (b, j, 0)),
        out_shape=jax.ShapeDtypeStruct((B, TT, D_MIX), BF16),
        compiler_params=_cparams(("arbitrary", "arbitrary")),
    )(oa_f, oa_b, ob_f, ob_b, oc, p, p, g_a.reshape(1, -1), g_b.reshape(1, -1), g_c.reshape(1, -1))


def kernel(x, c, ctx, c_ctx, w_ada, b_ada, norm_mix, w_in, hgrn_lower_bounds, ret_decay, diff_lambda, norm_a, norm_b, norm_c, w_out, norm_ffn, ffn_w1, ffn_w3, ffn_w2, router, moe_w1, moe_w3, moe_w2, final_norm):
    B, S, D = x.shape
    CTX = ctx.shape[1]
    L = w_ada.shape[0]
    TT = CTX + S
    M = B * TT
    F = ffn_w1.shape[-1]
    assert w_in.shape[-1] == D_PROJ and w_out.shape[1] == D_MIX
    assert B + 1 <= SUBLANES and CTX % CHUNK_B == 0 and S % CHUNK_B == 0 and S % GRID_W == 0

    lbs = jnp.cumsum(jax.nn.softmax(hgrn_lower_bounds.astype(F32), axis=0), axis=0)
    lbs = lbs - lbs[0:1]
    lb_tabs = jnp.stack([jnp.log(lbs), jnp.log1p(-lbs)], axis=2)
    log_gammas = jnp.log1p(-jnp.exp2(-ret_decay.astype(F32)))
    tabs = _rope_tables(S, CTX, DKB) + _rope_tables(S, CTX, DHC)
    tabs = tabs[:2] + tuple(jnp.concatenate([t, t], axis=-1) for t in tabs[2:])

    cvec = jnp.zeros((SUBLANES, D), F32).at[:B].set(c).at[B].set(c_ctx)
    mods = _ada_all(cvec, w_ada, b_ada)
    mods = mods.reshape(L, SUBLANES, 6, D).transpose(0, 2, 1, 3).reshape(L * 6 * SUBLANES, 1, D)
    mod_base = lambda l, chunk: (l * 6 + chunk) * SUBLANES

    bm = _pick(TT, (768, 384, 256, 128))
    bn_in = _pick(D_PROJ, (1280, 768, 512))
    bn_out = _pick(D, (1024, 512, 256))
    bf = _pick(F, (256, 128))
    bm_moe = _pick(M, (1056, 768, 384))
    bf_moe = _pick(F, (512, 256, 128))

    xc = jnp.concatenate([ctx, x], axis=1)
    for l in range(L):
        lam_init = 0.8 - 0.6 * math.exp(-0.3 * l)
        lq1, lk1, lq2, lk2 = diff_lambda[l].astype(F32)
        lam = (jnp.exp(jnp.sum(lq1 * lk1)) - jnp.exp(jnp.sum(lq2 * lk2)) + lam_init).reshape(1)

        h = _modulate(xc, norm_mix[l], mods, mod_base(l, 0), mod_base(l, 1), CTX)
        p = _matmul(h.reshape(M, D), w_in[l].astype(BF16), F32, bm, bn_in).reshape(B, TT, D_PROJ)
        oa_f, oa_b = _hgrn(p, lb_tabs[l], CTX)
        qb, kb, vb, qd, kd, vd = _prep(p, tabs, CTX)
        ob_f, ob_b = _retention(qb, kb, vb, log_gammas[l], CTX)
        oc = _diff_attention(qd, kd, vd, lam, CTX)
        y = _merge(oa_f, oa_b, ob_f, ob_b, oc, p, norm_a[l], norm_b[l], norm_c[l], lam_init, CTX)
        xc = _matmul_residual(y.reshape(M, D_MIX), w_out[l].astype(BF16), xc.reshape(M, D), mods,
                              mod_base(l, 2), B, CTX, bm, bn_out).reshape(B, TT, D)

        if l % 2 == 0:
            h = _modulate(xc, norm_ffn[l], mods, mod_base(l, 3), mod_base(l, 4), CTX)
            e = l // 2
            xc = _ffn(h.reshape(M, D), ffn_w1[e].astype(BF16), ffn_w3[e].astype(BF16), ffn_w2[e].astype(BF16),
                      xc.reshape(M, D), mods, mod_base(l, 5), B, CTX, bm, bf).reshape(B, TT, D)
        else:
            e = l // 2
            wr = jnp.zeros((D, LANES), F32).at[:, :N_EXPERTS].set(router[e])
            wr_hi = wr.astype(BF16)
            wr2 = jnp.stack([wr_hi, (wr - wr_hi.astype(F32)).astype(BF16)])
            h, gates = _modulate(xc, norm_ffn[l], mods, mod_base(l, 3), mod_base(l, 4), CTX, router=wr2)
            delta = _moe_sparse(h.reshape(M, D), gates.reshape(M, LANES), moe_w1[e].astype(BF16),
                                moe_w3[e].astype(BF16), moe_w2[e].astype(BF16), bm_moe, bf_moe)
            xc = _gated_residual(xc, delta.reshape(B, TT, D), mods, mod_base(l, 5), CTX)

    zeros = jnp.zeros((SUBLANES, 1, D), F32)
    return _modulate(xc, final_norm, zeros, 0, 0, 0, out_dtype=F32, row_offset=CTX, rows=S)
```

```python
import functools
import math

import numpy as np
import jax
import jax.numpy as jnp
from jax import lax
from jax.experimental import pallas as pl
from jax.experimental.pallas import tpu as pltpu

F32 = jnp.float32
BF16 = jnp.bfloat16

HA, DKA, DVA = 4, 128, 128
HB, DKB, DVB = 4, 128, 128
HC, DHC, DVC = 8, 64, 128
GRID_W = 64
N_EXPERTS = 8
ROPE_BASE = 10000.0
EPS = 1e-6
WA = HA * DKA
D_PROJ = 5 * WA + 4 * WA + 3 * HC * DVC
D_MIX = HA * DVA + HB * DVB + HC * DVC
COL_QA, COL_FF, COL_FB, COL_IA, COL_GA, COL_QB, COL_KB, COL_VB, COL_GB = range(9)
COL_QD, COL_KD, COL_VD = 9, 11, 13

LANES = 128
SUBLANES = 8
VMEM_LIMIT_MB = 56

CHUNK_A = 64
LEVELS_A = (32, 16, 8)
CHUNK_B = 256
NEG_BIG = -1e30
Q_SCALE_C = DHC ** -0.5 * math.log2(math.e)
Q_GROUP = 256
K_SUB = 256
SCORE_LOOKAHEAD = 2


def _cparams(semantics, vmem_mb=VMEM_LIMIT_MB):
    return pltpu.CompilerParams(dimension_semantics=semantics, vmem_limit_bytes=vmem_mb * 2 ** 20)


def _pick(n, candidates):
    for c in candidates:
        if n % c == 0:
            return c
    raise ValueError(f"no block size for {n} in {candidates}")


def _split3(x):
    hi = x.astype(BF16)
    r1 = x - hi.astype(F32)
    mid = r1.astype(BF16)
    lo = (r1 - mid.astype(F32)).astype(BF16)
    return hi, mid, lo


def _dot(a, b):
    return jnp.dot(a, b, preferred_element_type=F32)


def _dot_nt(a, b):
    return lax.dot_general(a, b, (((1,), (1,)), ((), ())), preferred_element_type=F32)


def _dot_tn(a, b):
    return lax.dot_general(a, b, (((0,), (0,)), ((), ())), preferred_element_type=F32)


def _ada_kernel(c_ref, w_ref, b_ref, o_ref):
    c = c_ref[...]
    s = c * jax.nn.sigmoid(c)
    o_ref[0] = _dot(s.astype(BF16), w_ref[0].astype(BF16)) + b_ref[0]


def _ada_all(cvec, w_ada, b_ada):
    L, D, N = w_ada.shape
    bn = _pick(N, (1024, 768, 512, 256, 128))
    return pl.pallas_call(
        _ada_kernel,
        grid=(L, N // bn),
        in_specs=[
            pl.BlockSpec((SUBLANES, D), lambda l, j: (0, 0)),
            pl.BlockSpec((1, D, bn), lambda l, j: (l, 0, j)),
            pl.BlockSpec((1, 1, bn), lambda l, j: (l, 0, j)),
        ],
        out_specs=pl.BlockSpec((1, SUBLANES, bn), lambda l, j: (l, 0, j)),
        out_shape=jax.ShapeDtypeStruct((L, SUBLANES, N), F32),
        compiler_params=_cparams(("arbitrary", "arbitrary")),
    )(cvec, w_ada, b_ada.reshape(L, 1, N))


def _rms(x):
    return x * lax.rsqrt(jnp.mean(x * x, axis=-1, keepdims=True) + EPS)


def _modulate_kernel(x_ref, g_ref, sh_ref, sc_ref, o_ref):
    y = _rms(x_ref[0]) * g_ref[...]
    o_ref[0] = (y * (1 + sc_ref[0]) + sh_ref[0]).astype(o_ref.dtype)


def _modulate_route_kernel(x_ref, g_ref, sh_ref, sc_ref, wr_ref, o_ref, gate_ref):
    y = _rms(x_ref[0]) * g_ref[...]
    h = y * (1 + sc_ref[0]) + sh_ref[0]
    o_ref[0] = h.astype(o_ref.dtype)
    hi = h.astype(BF16)
    lo = (h - hi.astype(F32)).astype(BF16)
    logits = _dot(hi, wr_ref[0]) + (_dot(lo, wr_ref[0]) + _dot(hi, wr_ref[1]))
    lane = lax.broadcasted_iota(jnp.int32, logits.shape, 1)
    logits = jnp.where(lane < N_EXPERTS, logits, -jnp.inf)
    m1 = jnp.max(logits, axis=-1, keepdims=True)
    i1 = jnp.min(jnp.where(logits == m1, lane, LANES), axis=-1, keepdims=True)
    rest = jnp.where(lane == i1, -jnp.inf, logits)
    m2 = jnp.max(rest, axis=-1, keepdims=True)
    i2 = jnp.min(jnp.where(rest == m2, lane, LANES), axis=-1, keepdims=True)
    e2 = jnp.exp(m2 - m1)
    w1 = 1.0 / (1.0 + e2)
    gate_ref[0] = jnp.where(lane == i1, w1, 0.0) + jnp.where(lane == i2, e2 * w1, 0.0)


def _mod_index(base, nctx_blocks):
    def index(b, j, *, nb):
        return (base + jnp.where(j < nctx_blocks, nb, b), 0, 0)
    return index


def _modulate(xc, g, mods, base_shift, base_scale, ctx, out_dtype=BF16, router=None, row_offset=0, rows=None):
    B, TT, D = xc.shape
    bt = _pick(ctx, (256, 128, 64))
    rows = TT if rows is None else rows
    off = row_offset // bt
    nctx = max(ctx // bt - off, 0)
    sh_idx = functools.partial(_mod_index(base_shift, nctx), nb=B)
    sc_idx = functools.partial(_mod_index(base_scale, nctx), nb=B)
    in_specs = [
        pl.BlockSpec((1, bt, D), lambda b, j: (b, j + off, 0)),
        pl.BlockSpec((1, D), lambda b, j: (0, 0)),
        pl.BlockSpec((1, 1, D), sh_idx),
        pl.BlockSpec((1, 1, D), sc_idx),
    ]
    out_spec = pl.BlockSpec((1, bt, D), lambda b, j: (b, j, 0))
    out_shape = jax.ShapeDtypeStruct((B, rows, D), out_dtype)
    if router is None:
        return pl.pallas_call(
            _modulate_kernel, grid=(B, rows // bt), in_specs=in_specs, out_specs=out_spec, out_shape=out_shape,
            compiler_params=_cparams(("arbitrary", "arbitrary")),
        )(xc, g.reshape(1, D), mods, mods)
    in_specs.append(pl.BlockSpec((2, D, LANES), lambda b, j: (0, 0, 0)))
    return pl.pallas_call(
        _modulate_route_kernel, grid=(B, rows // bt), in_specs=in_specs,
        out_specs=[out_spec, pl.BlockSpec((1, bt, LANES), lambda b, j: (b, j, 0))],
        out_shape=[out_shape, jax.ShapeDtypeStruct((B, rows, LANES), F32)],
        compiler_params=_cparams(("arbitrary", "arbitrary")),
    )(xc, g.reshape(1, D), mods, mods, router)


def _mm_kernel(a_ref, w_ref, o_ref):
    o_ref[...] = _dot(a_ref[...], w_ref[0]).astype(o_ref.dtype)


def _matmul(a, w, l, out_dtype, bm, bn):
    M, K = a.shape
    N = w.shape[2]
    return pl.pallas_call(
        _mm_kernel, grid=(M // bm, N // bn),
        in_specs=[pl.BlockSpec((bm, K), lambda i, j: (i, 0)), pl.BlockSpec((1, K, bn), lambda i, j: (l, 0, j))],
        out_specs=pl.BlockSpec((bm, bn), lambda i, j: (i, j)),
        out_shape=jax.ShapeDtypeStruct((M, N), out_dtype),
        compiler_params=_cparams(("arbitrary", "arbitrary")),
    )(a, w)


def _row_gate(shape, i, nb_per_batch, ctx, g_ctx, g_lat):
    nctx = jnp.where(i % nb_per_batch == 0, ctx, 0)
    rows = lax.broadcasted_iota(jnp.int32, shape, 0)
    return jnp.where(rows < nctx, g_ctx, g_lat)


def _mm_res_kernel(a_ref, w_ref, x_ref, gl_ref, gc_ref, o_ref, *, nb_per_batch, ctx):
    acc = _dot(a_ref[...], w_ref[0])
    g = _row_gate(acc.shape, pl.program_id(0), nb_per_batch, ctx, gc_ref[0], gl_ref[0])
    o_ref[...] = x_ref[...] + g * acc


def _matmul_residual(a, w, l, x2, mods, base_gate, B, ctx, bm, bn):
    M, K = a.shape
    N = w.shape[2]
    nbb = (M // B) // bm
    return pl.pallas_call(
        functools.partial(_mm_res_kernel, nb_per_batch=nbb, ctx=ctx),
        grid=(M // bm, N // bn),
        in_specs=[
            pl.BlockSpec((bm, K), lambda i, j: (i, 0)),
            pl.BlockSpec((1, K, bn), lambda i, j: (l, 0, j)),
            pl.BlockSpec((bm, bn), lambda i, j: (i, j)),
            pl.BlockSpec((1, 1, bn), lambda i, j: (base_gate + i // nbb, 0, j)),
            pl.BlockSpec((1, 1, bn), lambda i, j: (base_gate + B, 0, j)),
        ],
        out_specs=pl.BlockSpec((bm, bn), lambda i, j: (i, j)),
        out_shape=jax.ShapeDtypeStruct((M, N), F32),
        compiler_params=_cparams(("arbitrary", "arbitrary")),
    )(a, w, x2, mods, mods)


def _silu(x):
    return x * jax.nn.sigmoid(x)


def _ffn_kernel(h_ref, w1_ref, w3_ref, w2_ref, x_ref, gl_ref, gc_ref, o_ref, *, nb_per_batch, ctx, nf):
    f = pl.program_id(1)
    h = h_ref[...]
    hid = (_silu(_dot(h, w1_ref[0])) * _dot(h, w3_ref[0])).astype(BF16)
    part = _dot(hid, w2_ref[0])

    @pl.when(f == 0)
    def _():
        o_ref[...] = part

    @pl.when(f > 0)
    def _():
        o_ref[...] += part

    @pl.when(f == nf - 1)
    def _():
        g = _row_gate(o_ref.shape, pl.program_id(0), nb_per_batch, ctx, gc_ref[0], gl_ref[0])
        o_ref[...] = x_ref[...] + g * o_ref[...]


def _ffn(h, w1, w3, w2, l, x2, mods, base_gate, B, ctx, bm, bf):
    M, D = h.shape
    F = w1.shape[2]
    nf = F // bf
    nbb = (M // B) // bm
    return pl.pallas_call(
        functools.partial(_ffn_kernel, nb_per_batch=nbb, ctx=ctx, nf=nf),
        grid=(M // bm, nf),
        in_specs=[
            pl.BlockSpec((bm, D), lambda i, f: (i, 0)),
            pl.BlockSpec((1, D, bf), lambda i, f: (l, 0, f)),
            pl.BlockSpec((1, D, bf), lambda i, f: (l, 0, f)),
            pl.BlockSpec((1, bf, D), lambda i, f: (l, f, 0)),
            pl.BlockSpec((bm, D), lambda i, f: (i, 0)),
            pl.BlockSpec((1, 1, D), lambda i, f: (base_gate + i // nbb, 0, 0)),
            pl.BlockSpec((1, 1, D), lambda i, f: (base_gate + B, 0, 0)),
        ],
        out_specs=pl.BlockSpec((bm, D), lambda i, f: (i, 0)),
        out_shape=jax.ShapeDtypeStruct((M, D), F32),
        compiler_params=_cparams(("arbitrary", "arbitrary")),
    )(h, w1, w3, w2, x2, mods, mods)


def _moe_sparse_kernel(cnt_ref, h_ref, gate_ref, rank_ref, rankt_ref, w1_ref, w3_ref, w2_ref, o_ref,
                       hg_scr, acc_scr, *, nf, sub, first_slot, groups):
    i, e, f = pl.program_id(0), pl.program_id(1), pl.program_id(2)
    count = cnt_ref[i * N_EXPERTS + e]

    @pl.when((e == 0) & (f == 0))
    def _():
        o_ref[...] = jnp.zeros_like(o_ref)

    for g in range(groups):
        start = first_slot + g * sub
        active = count > start

        @pl.when(active & (f == 0))
        def _():
            slot = lax.broadcasted_iota(jnp.int32, (sub, h_ref.shape[0]), 0) + start
            pack = jnp.where(slot == rankt_ref[0, 0], 1.0, 0.0).astype(BF16)
            hg_scr[g] = _dot(pack, h_ref[...]).astype(BF16)
            acc_scr[g] = jnp.zeros((sub, h_ref.shape[1]), F32)

        @pl.when(active)
        def _():
            hg = hg_scr[g]
            hid = (_silu(_dot(hg, w1_ref[0, 0])) * _dot(hg, w3_ref[0, 0])).astype(BF16)
            acc_scr[g] += _dot(hid, w2_ref[0, 0])

        @pl.when(active & (f == nf - 1))
        def _():
            lane = lax.broadcasted_iota(jnp.int32, gate_ref.shape, 1)
            ge = jnp.sum(jnp.where(lane == e, gate_ref[...], 0.0), axis=-1, keepdims=True)
            rank = jnp.sum(jnp.where(lane == e, rank_ref[...], 0), axis=-1, keepdims=True)
            slot = lax.broadcasted_iota(jnp.int32, (h_ref.shape[0], sub), 1) + start
            unpack = jnp.where(slot == rank, 1.0, 0.0).astype(BF16)
            o_ref[...] += ge * _dot(unpack, acc_scr[g].astype(BF16))


def _moe_sparse(h, gates, w1, w3, w2, l, bm, bf):
    M, D = h.shape
    _, E, _, F = w1.shape
    nf, nblk = F // bf, M // bm
    sub = bm // 3
    routed = gates[:, :E].reshape(nblk, bm, E) > 0
    csum = jnp.cumsum(routed.astype(jnp.int32), axis=1)
    rank = jnp.where(routed, csum - 1, -1)
    counts = csum[:, -1, :].reshape(nblk * E)
    rank_lane = jnp.full((M, LANES), -1, jnp.int32).at[:, :E].set(rank.reshape(M, E))
    rank_t = rank.transpose(0, 2, 1).reshape(nblk, E, 1, bm)

    def run(first_slot, groups):
        return pl.pallas_call(
            functools.partial(_moe_sparse_kernel, nf=nf, sub=sub, first_slot=first_slot, groups=groups),
            grid_spec=pltpu.PrefetchScalarGridSpec(
                num_scalar_prefetch=1,
                grid=(nblk, E, nf),
                in_specs=[
                    pl.BlockSpec((bm, D), lambda i, e, f, c: (i, 0)),
                    pl.BlockSpec((bm, LANES), lambda i, e, f, c: (i, 0)),
                    pl.BlockSpec((bm, LANES), lambda i, e, f, c: (i, 0)),
                    pl.BlockSpec((1, 1, 1, bm), lambda i, e, f, c: (i, e, 0, 0)),
                    pl.BlockSpec((1, 1, D, bf), lambda i, e, f, c: (l, e, 0, f)),
                    pl.BlockSpec((1, 1, D, bf), lambda i, e, f, c: (l, e, 0, f)),
                    pl.BlockSpec((1, 1, bf, D), lambda i, e, f, c: (l, e, f, 0)),
                ],
                out_specs=pl.BlockSpec((bm, D), lambda i, e, f, c: (i, 0)),
                scratch_shapes=[pltpu.VMEM((groups, sub, D), BF16), pltpu.VMEM((groups, sub, D), F32)],
            ),
            out_shape=jax.ShapeDtypeStruct((M, D), F32),
            compiler_params=_cparams(("arbitrary", "arbitrary", "arbitrary")),
        )(counts, h, gates, rank_lane, rank_t, w1, w3, w2)

    delta = run(0, 2)
    return lax.cond(jnp.max(counts) > 2 * sub, lambda d: d + run(2 * sub, 1), lambda d: d, delta)


def _residual_kernel(x_ref, d_ref, g_ref, o_ref):
    o_ref[0] = x_ref[0] + g_ref[0] * d_ref[0]


def _gated_residual(xc, delta, mods, base_gate, ctx):
    B, TT, D = xc.shape
    bt = _pick(ctx, (256, 128, 64))
    blk = pl.BlockSpec((1, bt, D), lambda b, j: (b, j, 0))
    return pl.pallas_call(
        _residual_kernel, grid=(B, TT // bt),
        in_specs=[blk, blk, pl.BlockSpec((1, 1, D), functools.partial(_mod_index(base_gate, ctx // bt), nb=B))],
        out_specs=blk, out_shape=jax.ShapeDtypeStruct((B, TT, D), F32),
        compiler_params=_cparams(("arbitrary", "arbitrary")),
    )(xc, delta, mods)


def _rope_tables(seq, ctx, d):
    nf = d // 4
    rows = seq // GRID_W
    row = jnp.broadcast_to(jnp.arange(rows, dtype=jnp.int32)[:, None], (rows, GRID_W)).reshape(seq)
    col = jnp.broadcast_to(jnp.arange(GRID_W, dtype=jnp.int32)[None, :], (rows, GRID_W)).reshape(seq)
    inv_freq = ROPE_BASE ** (-jnp.arange(nf, dtype=F32) / nf)
    ar = row.astype(F32)[:, None] * inv_freq
    ac = col.astype(F32)[:, None] * inv_freq
    cos = jnp.concatenate([jnp.cos(ar), jnp.cos(ar), jnp.cos(ac), jnp.cos(ac)], axis=-1)
    sin = jnp.concatenate([-jnp.sin(ar), jnp.sin(ar), -jnp.sin(ac), jnp.sin(ac)], axis=-1)
    cos = jnp.concatenate([jnp.ones((ctx, d), F32), cos], axis=0)
    sin = jnp.concatenate([jnp.zeros((ctx, d), F32), sin], axis=0)
    return cos, sin


def _rope(x, cos, sin, quarter):
    lane = lax.broadcasted_iota(jnp.int32, x.shape, 1)
    up = pltpu.roll(x, LANES - quarter, 1)
    down = pltpu.roll(x, quarter, 1)
    swapped = jnp.where(lane % (2 * quarter) < quarter, up, down)
    return x * cos + swapped * sin


def _prep_kernel(qb_ref, kb_ref, vb_ref, qd0_ref, qd1_ref, kd0_ref, kd1_ref, vd0_ref, vd1_ref,
                 cb_ref, sb_ref, cc_ref, sc_ref, oqb, okb, ovb, oqd, okd, ovd):
    cb, sb, cc, sc = cb_ref[...], sb_ref[...], cc_ref[...], sc_ref[...]
    qb, kb = qb_ref[0], kb_ref[0]
    for h in range(HB):
        hs = slice(h * LANES, (h + 1) * LANES)
        oqb[0, :, hs] = _rope(qb[:, hs], cb, sb, DKB // 4).astype(BF16)
        okb[0, :, hs] = (_rope(kb[:, hs], cb, sb, DKB // 4) * DKB ** -0.5).astype(BF16)
    ovb[0] = vb_ref[0].astype(BF16)
    for half, (qr, kr, vr) in enumerate(((qd0_ref, kd0_ref, vd0_ref), (qd1_ref, kd1_ref, vd1_ref))):
        q, k, v = qr[0], kr[0], vr[0]
        for h in range(WA // LANES):
            hs = slice(h * LANES, (h + 1) * LANES)
            head = half * (WA // LANES) + h
            oqd[0, head] = (_rope(q[:, hs], cc, sc, DHC // 4) * Q_SCALE_C).T.astype(BF16)
            okd[0, :, half * WA + h * LANES:half * WA + (h + 1) * LANES] = _rope(k[:, hs], cc, sc, DHC // 4).astype(BF16)
            ovd[0, head] = v[:, hs].T.astype(BF16)


def _prep(p, tabs, ctx):
    B, TT, _ = p.shape
    bt = _pick(ctx, (256, 128, 64))
    pspec = lambda c: pl.BlockSpec((1, bt, WA), lambda b, j: (b, j, c))
    tspec = pl.BlockSpec((bt, LANES), lambda b, j: (j, 0))
    o512 = pl.BlockSpec((1, bt, WA), lambda b, j: (b, j, 0))
    o1024 = pl.BlockSpec((1, bt, 2 * WA), lambda b, j: (b, j, 0))
    o_t = pl.BlockSpec((1, HC, LANES, bt), lambda b, j: (b, 0, 0, j))
    s512 = jax.ShapeDtypeStruct((B, TT, WA), BF16)
    s1024 = jax.ShapeDtypeStruct((B, TT, 2 * WA), BF16)
    s_t = jax.ShapeDtypeStruct((B, HC, LANES, TT), BF16)
    cols = (COL_QB, COL_KB, COL_VB, COL_QD, COL_QD + 1, COL_KD, COL_KD + 1, COL_VD, COL_VD + 1)
    return pl.pallas_call(
        _prep_kernel, grid=(B, TT // bt),
        in_specs=[pspec(c) for c in cols] + [tspec] * 4,
        out_specs=[o512, o512, o512, o_t, o1024, o_t],
        out_shape=[s512, s512, s512, s_t, s1024, s_t],
        compiler_params=_cparams(("arbitrary", "arbitrary")),
    )(*([p] * 9), *tabs)


def _hgrn_consts():
    C = CHUNK_A
    t = np.arange(C)[:, None]
    u = np.arange(C)[None, :]
    out = []
    for rev in (False, True):
        mats = [(u >= t) if rev else (u <= t)]
        for m in LEVELS_A:
            r = (t // (2 * m)) * (2 * m) + (m if rev else m - 1)
            mats.append((u >= r) if rev else (u <= r))
        out.append(np.concatenate(mats, axis=0))
    return jnp.asarray(np.stack(out).astype(np.float32), dtype=BF16)


def _hgrn_kernel(qf_ref, zf_ref, vf_ref, qb_ref, zb_ref, vb_ref, cm_ref, lb_ref, of_ref, ob_ref, s_scr):
    C = CHUNK_A

    @pl.when(pl.program_id(1) == 0)
    def _():
        s_scr[...] = jnp.zeros_like(s_scr)

    row = lax.broadcasted_iota(jnp.int32, (C, LANES), 0)
    rr = lax.broadcasted_iota(jnp.int32, (C, C), 0)
    cc = lax.broadcasted_iota(jnp.int32, (C, C), 1)
    blk_row = lax.broadcasted_iota(jnp.int32, (C // SUBLANES, SUBLANES, LANES), 1)
    ones = jnp.ones((C, LANES), BF16)
    q_refs, z_refs, v_refs, o_refs = (qf_ref, qb_ref), (zf_ref, zb_ref), (vf_ref, vb_ref), (of_ref, ob_ref)

    ks, bbs, bend_ts = [], [], []
    for d in range(2):
        z = z_refs[d][0]
        llb, l1m = lb_ref[d, 0:1], lb_ref[d, 1:2]
        log_sig = jnp.minimum(z, 0.0) - jnp.log1p(jnp.exp(-jnp.abs(z)))
        t2 = l1m + log_sig
        logf = jnp.maximum(llb, t2) + jnp.log1p(jnp.exp(-jnp.abs(llb - t2)))
        ks.append(1.0 - jnp.exp(logf))
        parts = _split3(logf)
        cm = cm_ref[d]
        bbs.append(_dot(cm, parts[0]) + (_dot(cm, parts[1]) + _dot(cm, parts[2])))
        bend_ts.append(_dot_tn(parts[0], ones) + (_dot_tn(parts[1], ones) + _dot_tn(parts[2], ones)))

    heads = [(d, h) for d in range(2) for h in range(HA)]

    def operands(d, h):
        hs = slice(h * LANES, (h + 1) * LANES)
        return q_refs[d][0, :, hs], ks[d][:, hs], v_refs[d][0, :, hs], bbs[d][:C, hs]

    level_dots = {}
    for d, h in heads:
        q, k, _, b = operands(d, h)
        dots = []
        for lvl, m in enumerate(LEVELS_A):
            bref = bbs[d][(lvl + 1) * C:(lvl + 2) * C, h * LANES:(h + 1) * LANES]
            e = jnp.exp(-jnp.abs(b - bref))
            upper = (row % (2 * m)) >= m
            q_side = jnp.logical_not(upper) if d else upper
            qt = jnp.where(q_side, q * e, 0.0).astype(BF16)
            kt = jnp.where(q_side, 0.0, k * e).astype(BF16)
            dots.append(_dot_nt(qt, kt))
        level_dots[d, h] = dots

    diag = {}
    for d, h in heads:
        q3, k3, v3, b3 = (x.reshape(C // SUBLANES, SUBLANES, LANES) for x in operands(d, h))
        acc = jnp.zeros_like(q3)
        for j in range(SUBLANES):
            kj, vj, bj = k3[:, j:j + 1], v3[:, j:j + 1], b3[:, j:j + 1]
            e = jnp.exp(jnp.minimum(b3 - bj, 0.0))
            a = jnp.sum(q3 * kj * e, axis=-1, keepdims=True)
            valid = (blk_row <= j) if d else (blk_row >= j)
            acc = acc + jnp.where(valid, a, 0.0) * vj
        diag[d, h] = acc.reshape(C, LANES)

    outs, updates = {}, {}
    for d, h in heads:
        q, k, v, b = operands(d, h)
        v16 = v.astype(BF16)
        s_old = s_scr[d, h]
        att = jnp.zeros((C, C), F32)
        for m, a in zip(LEVELS_A, level_dots[d, h]):
            att = att + jnp.where((rr // (2 * m)) == (cc // (2 * m)), a, 0.0)
        inter = _dot((q * jnp.exp(b)).astype(BF16), s_old.astype(BF16))
        outs[d, h] = inter + _dot(att.astype(BF16), v16)
        end = 0 if d else C - 1
        bend = b[end:end + 1, :]
        kdec = (k * jnp.exp(bend - b)).astype(BF16)
        updates[d, h] = jnp.exp(bend_ts[d][h * LANES:(h + 1) * LANES, :]) * s_old + _dot_tn(kdec, v16)

    for d, h in heads:
        o_refs[d][0, :, h * LANES:(h + 1) * LANES] = outs[d, h] + diag[d, h]
        s_scr[d, h] = updates[d, h]


def _bwd_chunk(i, nctx, n):
    return jnp.where(i < nctx, nctx - 1 - i, n - 1 + nctx - i)


def _hgrn(p, lb_tab, ctx):
    B, TT, _ = p.shape
    C = CHUNK_A
    n, nctx = TT // C, ctx // C
    fwd = lambda c: pl.BlockSpec((1, C, WA), lambda b, i: (b, i, c))
    bwd = lambda c: pl.BlockSpec((1, C, WA), lambda b, i: (b, _bwd_chunk(i, nctx, n), c))
    cm = _hgrn_consts()
    shape = jax.ShapeDtypeStruct((B, TT, WA), F32)
    return pl.pallas_call(
        _hgrn_kernel, grid=(B, n),
        in_specs=[fwd(COL_QA), fwd(COL_FF), fwd(COL_IA), bwd(COL_QA), bwd(COL_FB), bwd(COL_IA),
                  pl.BlockSpec(cm.shape, lambda b, i: (0, 0, 0)),
                  pl.BlockSpec((2, 2, WA), lambda b, i: (0, 0, 0))],
        out_specs=[pl.BlockSpec((1, C, WA), lambda b, i: (b, i, 0)),
                   pl.BlockSpec((1, C, WA), lambda b, i: (b, _bwd_chunk(i, nctx, n), 0))],
        out_shape=[shape, shape],
        scratch_shapes=[pltpu.VMEM((2, HA, DKA, DVA), F32)],
        compiler_params=_cparams(("arbitrary", "arbitrary")),
    )(p, p, p, p, p, p, cm, lb_tab)


def _ret_kernel(lg_ref, qf_ref, kf_ref, vf_ref, qb_ref, kb_ref, vb_ref, of_ref, ob_ref,
                s_scr, d_scr, rq_scr, rk_scr, gc_scr):
    C = CHUNK_B

    @pl.when((pl.program_id(0) == 0) & (pl.program_id(1) == 0))
    def _():
        t = lax.broadcasted_iota(jnp.int32, (C, C), 0)
        s = lax.broadcasted_iota(jnp.int32, (C, C), 1)
        pos = lax.broadcasted_iota(jnp.int32, (C, LANES), 0).astype(F32)
        for d in range(2):
            for h in range(HB):
                lg = lg_ref[d, h]
                delta = (s - t) if d else (t - s)
                d_scr[d, h] = jnp.where(delta >= 0, jnp.exp(jnp.maximum(delta, 0).astype(F32) * lg), 0.0)
                rq_scr[d, h] = jnp.exp(((C - pos) if d else (pos + 1.0)) * lg)
                rk_scr[d, h] = jnp.exp((pos if d else (C - 1.0 - pos)) * lg)
                gc_scr[d, h] = jnp.exp(jnp.full((SUBLANES, LANES), C, F32) * lg)

    @pl.when(pl.program_id(1) == 0)
    def _():
        s_scr[...] = jnp.zeros_like(s_scr)

    for d, (q_ref, k_ref, v_ref, o_ref) in enumerate(((qf_ref, kf_ref, vf_ref, of_ref), (qb_ref, kb_ref, vb_ref, ob_ref))):
        for h in range(HB):
            hs = slice(h * LANES, (h + 1) * LANES)
            q, k, v = q_ref[0, :, hs], k_ref[0, :, hs], v_ref[0, :, hs]
            att = (_dot_nt(q, k) * d_scr[d, h]).astype(BF16)
            s_old = s_scr[d, h]
            o_ref[0, :, hs] = _dot(att, v) + rq_scr[d, h] * _dot(q, s_old.astype(BF16))
            kdec = (k.astype(F32) * rk_scr[d, h]).astype(BF16)
            s_scr[d, h] = gc_scr[d, h][0:1, :] * s_old + _dot_tn(kdec, v)


def _retention(qb, kb, vb, log_gamma, ctx):
    B, TT, _ = qb.shape
    C = CHUNK_B
    n, nctx = TT // C, ctx // C
    fwd = pl.BlockSpec((1, C, WA), lambda b, i: (b, i, 0))
    bwd = pl.BlockSpec((1, C, WA), lambda b, i: (b, _bwd_chunk(i, nctx, n), 0))
    shape = jax.ShapeDtypeStruct((B, TT, WA), F32)
    return pl.pallas_call(
        _ret_kernel, grid=(B, n),
        in_specs=[pl.BlockSpec(memory_space=pltpu.SMEM), fwd, fwd, fwd, bwd, bwd, bwd],
        out_specs=[fwd, bwd], out_shape=[shape, shape],
        scratch_shapes=[pltpu.VMEM((2, HB, DKB, DVB), F32), pltpu.VMEM((2, HB, C, C), F32),
                        pltpu.VMEM((2, HB, C, LANES), F32), pltpu.VMEM((2, HB, C, LANES), F32),
                        pltpu.VMEM((2, HB, SUBLANES, LANES), F32)],
        compiler_params=_cparams(("arbitrary", "arbitrary")),
    )(log_gamma, qb, kb, vb, qb, kb, vb)


def _attn_kernel(lam_ref, q_ref, k_ref, v_ref, o_ref, qm_scr, m_scr, l_scr, acc_scr, *, ctx, tq, tk, nk):
    qi = pl.program_id(2)
    ki = pl.program_id(3)

    @pl.when(ki == 0)
    def _():
        q = q_ref[0, 0]
        row = lax.broadcasted_iota(jnp.int32, q.shape, 0)
        zero = jnp.zeros_like(q)
        qm_scr[0] = jnp.where(row < DHC, q, zero)
        qm_scr[1] = jnp.where(row < DHC, zero, q)
        m_scr[...] = jnp.full_like(m_scr, NEG_BIG)
        l_scr[...] = jnp.zeros_like(l_scr)
        acc_scr[...] = jnp.zeros_like(acc_scr)

    def step(masked):
        k = k_ref[0]
        vt = v_ref[0, 0]
        m_old, l_old, acc_old = m_scr[...], l_scr[...], acc_scr[...]
        key_subs = [slice(j, j + K_SUB) for j in range(0, tk, K_SUB)]
        m_out, l_out, acc_out = [[], []], [[], []], [[], []]
        chains = [(g, c) for g in range(tq // Q_GROUP) for c in range(2)]

        def scores(g, c):
            qg = qm_scr[c, :, g * Q_GROUP:(g + 1) * Q_GROUP]
            s = [_dot(k[ks], qg) for ks in key_subs]
            if masked and g * Q_GROUP < ctx:
                keys = lax.broadcasted_iota(jnp.int32, (tk, Q_GROUP), 0) + ki * tk
                qpos = lax.broadcasted_iota(jnp.int32, (tk, Q_GROUP), 1) + g * Q_GROUP
                hidden = (qpos < ctx) & (keys >= ctx)
                s = [jnp.where(hidden[ks], NEG_BIG, sj) for ks, sj in zip(key_subs, s)]
            return s

        def finish(g, c, s):
            gs = slice(g * Q_GROUP, (g + 1) * Q_GROUP)
            m_prev = m_old[c, :, gs]
            m_new = functools.reduce(jnp.maximum, [jnp.max(sj, axis=0, keepdims=True) for sj in s], m_prev)
            alpha = jnp.exp2(m_prev - m_new)
            p = [jnp.exp2(sj - m_new) for sj in s]
            l_new = functools.reduce(jnp.add, [jnp.sum(pj, axis=0, keepdims=True) for pj in p])
            pv = functools.reduce(jnp.add, [_dot(vt[:, ks], pj.astype(BF16)) for ks, pj in zip(key_subs, p)])
            l_out[c].append(alpha * l_old[c, :, gs] + l_new)
            acc_out[c].append(alpha * acc_old[c, :, gs] + pv)
            m_out[c].append(m_new)

        pending = [scores(*ch) for ch in chains[:SCORE_LOOKAHEAD]]
        for n, chain in enumerate(chains):
            if n + SCORE_LOOKAHEAD < len(chains):
                pending.append(scores(*chains[n + SCORE_LOOKAHEAD]))
            finish(*chain, pending.pop(0))
        for c in range(2):
            m_scr[c] = jnp.concatenate(m_out[c], axis=-1)
            l_scr[c] = jnp.concatenate(l_out[c], axis=-1)
            acc_scr[c] = jnp.concatenate(acc_out[c], axis=-1)

    @pl.when(qi == 0)
    def _():
        step(True)

    @pl.when(qi > 0)
    def _():
        step(False)

    @pl.when(ki == nk - 1)
    def _():
        o_t = acc_scr[0] / l_scr[0] - lam_ref[0] * (acc_scr[1] / l_scr[1])
        o_ref[0] = o_t.T


def _diff_attention(qd, kd, vd, lam, ctx):
    B, TT, _ = kd.shape
    tq = _pick(TT, (2816, 768, 512, 256))
    tk = _pick(TT, (768, 512, 256))
    assert ctx <= tk and ctx <= tq and tq % Q_GROUP == 0 and tk % K_SUB == 0
    nq, nk = TT // tq, TT // tk
    return pl.pallas_call(
        functools.partial(_attn_kernel, ctx=ctx, tq=tq, tk=tk, nk=nk),
        grid=(B, HC, nq, nk),
        in_specs=[pl.BlockSpec(memory_space=pltpu.SMEM),
                  pl.BlockSpec((1, 1, LANES, tq), lambda b, h, i, j: (b, h, 0, i)),
                  pl.BlockSpec((1, tk, LANES), lambda b, h, i, j: (b, j, h)),
                  pl.BlockSpec((1, 1, LANES, tk), lambda b, h, i, j: (b, h, 0, j))],
        out_specs=pl.BlockSpec((1, tq, LANES), lambda b, h, i, j: (b, i, h)),
        out_shape=jax.ShapeDtypeStruct((B, TT, HC * DVC), F32),
        scratch_shapes=[pltpu.VMEM((2, LANES, tq), BF16), pltpu.VMEM((2, 1, tq), F32),
                        pltpu.VMEM((2, 1, tq), F32), pltpu.VMEM((2, DVC, tq), F32)],
        compiler_params=_cparams(("arbitrary", "arbitrary", "arbitrary", "arbitrary")),
    )(lam, qd, kd, vd)


def _merge_kernel(af_ref, ab_ref, bf_ref, bb_ref, c_ref, ga_ref, gb_ref, na_ref, nb_ref, nc_ref, o_ref, *, c_scale):
    oa = af_ref[0] + ab_ref[0]
    ob = bf_ref[0] + bb_ref[0]
    oc = c_ref[0]
    ga, gb = ga_ref[0], gb_ref[0]
    for h in range(HA):
        hs = slice(h * LANES, (h + 1) * LANES)
        o_ref[0, :, hs] = (_rms(oa[:, hs]) * na_ref[:, hs] * _silu(ga[:, hs])).astype(o_ref.dtype)
    for h in range(HB):
        hs = slice(h * LANES, (h + 1) * LANES)
        os_ = slice(WA + h * LANES, WA + (h + 1) * LANES)
        o_ref[0, :, os_] = (_rms(ob[:, hs]) * nb_ref[:, hs] * _silu(gb[:, hs])).astype(o_ref.dtype)
    for h in range(HC):
        hs = slice(h * LANES, (h + 1) * LANES)
        os_ = slice(2 * WA + h * LANES, 2 * WA + (h + 1) * LANES)
        o_ref[0, :, os_] = (_rms(oc[:, hs]) * nc_ref[:, hs] * c_scale).astype(o_ref.dtype)


def _merge(oa_f, oa_b, ob_f, ob_b, oc, p, g_a, g_b, g_c, lam_init, ctx):
    B, TT, _ = oc.shape
    bt = _pick(ctx, (256, 128, 64))
    s512 = pl.BlockSpec((1, bt, WA), lambda b, j: (b, j, 0))
    pcol = lambda c: pl.BlockSpec((1, bt, WA), lambda b, j: (b, j, c))
    return pl.pallas_call(
        functools.partial(_merge_kernel, c_scale=1.0 - lam_init),
        grid=(B, TT // bt),
        in_specs=[s512, s512, s512, s512, pl.BlockSpec((1, bt, 2 * WA), lambda b, j: (b, j, 0)),
                  pcol(COL_GA), pcol(COL_GB),
                  pl.BlockSpec((1, WA), lambda b, j: (0, 0)), pl.BlockSpec((1, WA), lambda b, j: (0, 0)),
                  pl.BlockSpec((1, 2 * WA), lambda b, j: (0, 0))],
        out_specs=pl.BlockSpec((1, bt, D_MIX), lambda b, j: (b, j, 0)),
        out_shape=jax.ShapeDtypeStruct((B, TT, D_MIX), BF16),
        compiler_params=_cparams(("arbitrary", "arbitrary")),
    )(oa_f, oa_b, ob_f, ob_b, oc, p, p, g_a.reshape(1, -1), g_b.reshape(1, -1), g_c.reshape(1, -1))


def kernel(x, c, ctx, c_ctx, w_ada, b_ada, norm_mix, w_in, hgrn_lower_bounds, ret_decay, diff_lambda, norm_a, norm_b, norm_c, w_out, norm_ffn, ffn_w1, ffn_w3, ffn_w2, router, moe_w1, moe_w3, moe_w2, final_norm):
    B, S, D = x.shape
    CTX = ctx.shape[1]
    L = w_ada.shape[0]
    TT = CTX + S
    M = B * TT
    F = ffn_w1.shape[-1]
    assert w_in.shape[-1] == D_PROJ and w_out.shape[1] == D_MIX
    assert B + 1 <= SUBLANES and CTX % CHUNK_B == 0 and S % CHUNK_B == 0 and S % GRID_W == 0

    lbs = jnp.cumsum(jax.nn.softmax(hgrn_lower_bounds.astype(F32), axis=0), axis=0)
    lbs = lbs - lbs[0:1]
    lb_tabs = jnp.stack([jnp.log(lbs), jnp.log1p(-lbs)], axis=2)
    log_gammas = jnp.log1p(-jnp.exp2(-ret_decay.astype(F32)))
    tabs = _rope_tables(S, CTX, DKB) + _rope_tables(S, CTX, DHC)
    tabs = tabs[:2] + tuple(jnp.concatenate([t, t], axis=-1) for t in tabs[2:])

    cvec = jnp.zeros((SUBLANES, D), F32).at[:B].set(c).at[B].set(c_ctx)
    mods = _ada_all(cvec, w_ada, b_ada)
    mods = mods.reshape(L, SUBLANES, 6, D).transpose(0, 2, 1, 3).reshape(L * 6 * SUBLANES, 1, D)
    mod_base = lambda l, chunk: (l * 6 + chunk) * SUBLANES

    bm = _pick(TT, (768, 384, 256, 128))
    bn_in = _pick(D_PROJ, (1280, 768, 512))
    bn_out = _pick(D, (1024, 512, 256))
    bf = _pick(F, (256, 128))
    bm_moe = _pick(M, (1056, 768, 384))
    bf_moe = _pick(F, (512, 256, 128))

    w_in16, w_out16 = w_in.astype(BF16), w_out.astype(BF16)
    ffn16 = tuple(w.astype(BF16) for w in (ffn_w1, ffn_w3, ffn_w2))
    moe16 = tuple(w.astype(BF16) for w in (moe_w1, moe_w3, moe_w2))

    xc = jnp.concatenate([ctx, x], axis=1)
    for l in range(L):
        lam_init = 0.8 - 0.6 * math.exp(-0.3 * l)
        lq1, lk1, lq2, lk2 = diff_lambda[l].astype(F32)
        lam = (jnp.exp(jnp.sum(lq1 * lk1)) - jnp.exp(jnp.sum(lq2 * lk2)) + lam_init).reshape(1)

        h = _modulate(xc, norm_mix[l], mods, mod_base(l, 0), mod_base(l, 1), CTX)
        p = _matmul(h.reshape(M, D), w_in16, l, F32, bm, bn_in).reshape(B, TT, D_PROJ)
        oa_f, oa_b = _hgrn(p, lb_tabs[l], CTX)
        qb, kb, vb, qd, kd, vd = _prep(p, tabs, CTX)
        ob_f, ob_b = _retention(qb, kb, vb, log_gammas[l], CTX)
        oc = _diff_attention(qd, kd, vd, lam, CTX)
        y = _merge(oa_f, oa_b, ob_f, ob_b, oc, p, norm_a[l], norm_b[l], norm_c[l], lam_init, CTX)
        xc = _matmul_residual(y.reshape(M, D_MIX), w_out16, l, xc.reshape(M, D), mods,
                              mod_base(l, 2), B, CTX, bm, bn_out).reshape(B, TT, D)

        if l % 2 == 0:
            h = _modulate(xc, norm_ffn[l], mods, mod_base(l, 3), mod_base(l, 4), CTX)
            e = l // 2
            xc = _ffn(h.reshape(M, D), *ffn16, e, xc.reshape(M, D), mods, mod_base(l, 5), B, CTX, bm, bf).reshape(B, TT, D)
        else:
            e = l // 2
            wr = jnp.zeros((D, LANES), F32).at[:, :N_EXPERTS].set(router[e])
            wr_hi = wr.astype(BF16)
            wr2 = jnp.stack([wr_hi, (wr - wr_hi.astype(F32)).astype(BF16)])
            h, gates = _modulate(xc, norm_ffn[l], mods, mod_base(l, 3), mod_base(l, 4), CTX, router=wr2)
            delta = _moe_sparse(h.reshape(M, D), gates.reshape(M, LANES), *moe16, e, bm_moe, bf_moe)
            xc = _gated_residual(xc, delta.reshape(B, TT, D), mods, mod_base(l, 5), CTX)

    zeros = jnp.zeros((SUBLANES, 1, D), F32)
    return _modulate(xc, final_norm, zeros, 0, 0, 0, out_dtype=F32, row_offset=CTX, rows=S)
```

```python
import functools
import math

import numpy as np
import jax
import jax.numpy as jnp
from jax import lax
from jax.experimental import pallas as pl
from jax.experimental.pallas import tpu as pltpu

F32 = jnp.float32
BF16 = jnp.bfloat16

HA, DKA, DVA = 4, 128, 128
HB, DKB, DVB = 4, 128, 128
HC, DHC, DVC = 8, 64, 128
GRID_W = 64
N_EXPERTS = 8
ROPE_BASE = 10000.0
EPS = 1e-6
WA = HA * DKA
D_PROJ = 5 * WA + 4 * WA + 3 * HC * DVC
D_MIX = HA * DVA + HB * DVB + HC * DVC
COL_QA, COL_FF, COL_FB, COL_IA, COL_GA, COL_QB, COL_KB, COL_VB, COL_GB = range(9)
COL_QD, COL_KD, COL_VD = 9, 11, 13

LANES = 128
SUBLANES = 8
VMEM_LIMIT_MB = 56

CHUNK_A = 64
LEVELS_A = (32, 16, 8)
CHUNK_B = 256
NEG_BIG = -1e30
Q_SCALE_C = DHC ** -0.5 * math.log2(math.e)
Q_GROUP = 256
K_SUB = 256
SCORE_LOOKAHEAD = 3


def _cparams(semantics, vmem_mb=VMEM_LIMIT_MB):
    return pltpu.CompilerParams(dimension_semantics=semantics, vmem_limit_bytes=vmem_mb * 2 ** 20)


def _pick(n, candidates):
    for c in candidates:
        if n % c == 0:
            return c
    raise ValueError(f"no block size for {n} in {candidates}")


def _split3(x):
    hi = x.astype(BF16)
    r1 = x - hi.astype(F32)
    mid = r1.astype(BF16)
    lo = (r1 - mid.astype(F32)).astype(BF16)
    return hi, mid, lo


def _dot(a, b):
    return jnp.dot(a, b, preferred_element_type=F32)


def _dot_nt(a, b):
    return lax.dot_general(a, b, (((1,), (1,)), ((), ())), preferred_element_type=F32)


def _dot_tn(a, b):
    return lax.dot_general(a, b, (((0,), (0,)), ((), ())), preferred_element_type=F32)


def _ada_kernel(c_ref, w_ref, b_ref, o_ref):
    c = c_ref[...]
    s = c * jax.nn.sigmoid(c)
    o_ref[0] = _dot(s.astype(BF16), w_ref[0].astype(BF16)) + b_ref[0]


def _ada_all(cvec, w_ada, b_ada):
    L, D, N = w_ada.shape
    bn = _pick(N, (1024, 768, 512, 256, 128))
    return pl.pallas_call(
        _ada_kernel,
        grid=(L, N // bn),
        in_specs=[
            pl.BlockSpec((SUBLANES, D), lambda l, j: (0, 0)),
            pl.BlockSpec((1, D, bn), lambda l, j: (l, 0, j)),
            pl.BlockSpec((1, 1, bn), lambda l, j: (l, 0, j)),
        ],
        out_specs=pl.BlockSpec((1, SUBLANES, bn), lambda l, j: (l, 0, j)),
        out_shape=jax.ShapeDtypeStruct((L, SUBLANES, N), F32),
        compiler_params=_cparams(("arbitrary", "arbitrary")),
    )(cvec, w_ada, b_ada.reshape(L, 1, N))


def _rms(x):
    return x * lax.rsqrt(jnp.mean(x * x, axis=-1, keepdims=True) + EPS)


def _modulate_kernel(x_ref, g_ref, sh_ref, sc_ref, o_ref):
    y = _rms(x_ref[0]) * g_ref[...]
    o_ref[0] = (y * (1 + sc_ref[0]) + sh_ref[0]).astype(o_ref.dtype)


def _modulate_route_kernel(x_ref, g_ref, sh_ref, sc_ref, wr_ref, o_ref, gate_ref):
    y = _rms(x_ref[0]) * g_ref[...]
    h = y * (1 + sc_ref[0]) + sh_ref[0]
    o_ref[0] = h.astype(o_ref.dtype)
    hi = h.astype(BF16)
    lo = (h - hi.astype(F32)).astype(BF16)
    logits = _dot(hi, wr_ref[0]) + (_dot(lo, wr_ref[0]) + _dot(hi, wr_ref[1]))
    lane = lax.broadcasted_iota(jnp.int32, logits.shape, 1)
    logits = jnp.where(lane < N_EXPERTS, logits, -jnp.inf)
    m1 = jnp.max(logits, axis=-1, keepdims=True)
    i1 = jnp.min(jnp.where(logits == m1, lane, LANES), axis=-1, keepdims=True)
    rest = jnp.where(lane == i1, -jnp.inf, logits)
    m2 = jnp.max(rest, axis=-1, keepdims=True)
    i2 = jnp.min(jnp.where(rest == m2, lane, LANES), axis=-1, keepdims=True)
    e2 = jnp.exp(m2 - m1)
    w1 = 1.0 / (1.0 + e2)
    gate_ref[0] = jnp.where(lane == i1, w1, 0.0) + jnp.where(lane == i2, e2 * w1, 0.0)


def _mod_index(base, nctx_blocks):
    def index(b, j, *, nb):
        return (base + jnp.where(j < nctx_blocks, nb, b), 0, 0)
    return index


def _modulate(xc, g, mods, base_shift, base_scale, ctx, out_dtype=BF16, router=None, row_offset=0, rows=None):
    B, TT, D = xc.shape
    bt = _pick(ctx, (256, 128, 64))
    rows = TT if rows is None else rows
    off = row_offset // bt
    nctx = max(ctx // bt - off, 0)
    sh_idx = functools.partial(_mod_index(base_shift, nctx), nb=B)
    sc_idx = functools.partial(_mod_index(base_scale, nctx), nb=B)
    in_specs = [
        pl.BlockSpec((1, bt, D), lambda b, j: (b, j + off, 0)),
        pl.BlockSpec((1, D), lambda b, j: (0, 0)),
        pl.BlockSpec((1, 1, D), sh_idx),
        pl.BlockSpec((1, 1, D), sc_idx),
    ]
    out_spec = pl.BlockSpec((1, bt, D), lambda b, j: (b, j, 0))
    out_shape = jax.ShapeDtypeStruct((B, rows, D), out_dtype)
    if router is None:
        return pl.pallas_call(
            _modulate_kernel, grid=(B, rows // bt), in_specs=in_specs, out_specs=out_spec, out_shape=out_shape,
            compiler_params=_cparams(("arbitrary", "arbitrary")),
        )(xc, g.reshape(1, D), mods, mods)
    in_specs.append(pl.BlockSpec((2, D, LANES), lambda b, j: (0, 0, 0)))
    return pl.pallas_call(
        _modulate_route_kernel, grid=(B, rows // bt), in_specs=in_specs,
        out_specs=[out_spec, pl.BlockSpec((1, bt, LANES), lambda b, j: (b, j, 0))],
        out_shape=[out_shape, jax.ShapeDtypeStruct((B, rows, LANES), F32)],
        compiler_params=_cparams(("arbitrary", "arbitrary")),
    )(xc, g.reshape(1, D), mods, mods, router)


def _row_gate(shape, i, nb_per_batch, ctx, g_ctx, g_lat):
    nctx = jnp.where(i % nb_per_batch == 0, ctx, 0)
    rows = lax.broadcasted_iota(jnp.int32, shape, 0)
    return jnp.where(rows < nctx, g_ctx, g_lat)


def _modulated(x, g, shl_ref, shc_ref, scl_ref, scc_ref, i, nb_per_batch, ctx):
    shift = _row_gate(x.shape, i, nb_per_batch, ctx, shc_ref[0], shl_ref[0])
    scale = _row_gate(x.shape, i, nb_per_batch, ctx, scc_ref[0], scl_ref[0])
    return _rms(x) * g * (1 + scale) + shift


def _mod_specs(base_shift, base_scale, nbb, B, D):
    def spec(base, ctx_row):
        return pl.BlockSpec((1, 1, D), lambda i, j: (base + (B if ctx_row else i // nbb), 0, 0))
    return [spec(base_shift, False), spec(base_shift, True), spec(base_scale, False), spec(base_scale, True)]


def _mm_mod_kernel(x_ref, g_ref, shl_ref, shc_ref, scl_ref, scc_ref, w_ref, o_ref, h_scr, *, nb_per_batch, ctx):
    @pl.when(pl.program_id(1) == 0)
    def _():
        h = _modulated(x_ref[...], g_ref[...], shl_ref, shc_ref, scl_ref, scc_ref, pl.program_id(0), nb_per_batch, ctx)
        h_scr[...] = h.astype(BF16)

    o_ref[...] = _dot(h_scr[...], w_ref[0]).astype(o_ref.dtype)


def _modulated_matmul(x2, g, mods, base_shift, base_scale, w, l, B, ctx, bm, bn):
    M, K = x2.shape
    N = w.shape[2]
    nbb = (M // B) // bm
    return pl.pallas_call(
        functools.partial(_mm_mod_kernel, nb_per_batch=nbb, ctx=ctx), grid=(M // bm, N // bn),
        in_specs=[pl.BlockSpec((bm, K), lambda i, j: (i, 0)), pl.BlockSpec((1, K), lambda i, j: (0, 0))]
        + _mod_specs(base_shift, base_scale, nbb, B, K)
        + [pl.BlockSpec((1, K, bn), lambda i, j: (l, 0, j))],
        out_specs=pl.BlockSpec((bm, bn), lambda i, j: (i, j)),
        out_shape=jax.ShapeDtypeStruct((M, N), F32),
        scratch_shapes=[pltpu.VMEM((bm, K), BF16)],
        compiler_params=_cparams(("arbitrary", "arbitrary")),
    )(x2, g.reshape(1, K), mods, mods, mods, mods, w)


def _mm_res_kernel(a_ref, w_ref, x_ref, gl_ref, gc_ref, o_ref, *, nb_per_batch, ctx):
    acc = _dot(a_ref[...], w_ref[0])
    g = _row_gate(acc.shape, pl.program_id(0), nb_per_batch, ctx, gc_ref[0], gl_ref[0])
    o_ref[...] = x_ref[...] + g * acc


def _matmul_residual(a, w, l, x2, mods, base_gate, B, ctx, bm, bn):
    M, K = a.shape
    N = w.shape[2]
    nbb = (M // B) // bm
    return pl.pallas_call(
        functools.partial(_mm_res_kernel, nb_per_batch=nbb, ctx=ctx),
        grid=(M // bm, N // bn),
        in_specs=[
            pl.BlockSpec((bm, K), lambda i, j: (i, 0)),
            pl.BlockSpec((1, K, bn), lambda i, j: (l, 0, j)),
            pl.BlockSpec((bm, bn), lambda i, j: (i, j)),
            pl.BlockSpec((1, 1, bn), lambda i, j: (base_gate + i // nbb, 0, j)),
            pl.BlockSpec((1, 1, bn), lambda i, j: (base_gate + B, 0, j)),
        ],
        out_specs=pl.BlockSpec((bm, bn), lambda i, j: (i, j)),
        out_shape=jax.ShapeDtypeStruct((M, N), F32),
        compiler_params=_cparams(("arbitrary", "arbitrary")),
    )(a, w, x2, mods, mods)


def _silu(x):
    return x * jax.nn.sigmoid(x)


def _ffn_kernel(x_ref, g_ref, shl_ref, shc_ref, scl_ref, scc_ref, w1_ref, w3_ref, w2_ref, gl_ref, gc_ref, o_ref,
                h_scr, *, nb_per_batch, ctx, nf):
    f = pl.program_id(1)

    @pl.when(f == 0)
    def _():
        h = _modulated(x_ref[...], g_ref[...], shl_ref, shc_ref, scl_ref, scc_ref, pl.program_id(0), nb_per_batch, ctx)
        h_scr[...] = h.astype(BF16)
        o_ref[...] = jnp.zeros_like(o_ref)

    h = h_scr[...]
    hid = (_silu(_dot(h, w1_ref[0])) * _dot(h, w3_ref[0])).astype(BF16)
    o_ref[...] += _dot(hid, w2_ref[0])

    @pl.when(f == nf - 1)
    def _():
        g = _row_gate(o_ref.shape, pl.program_id(0), nb_per_batch, ctx, gc_ref[0], gl_ref[0])
        o_ref[...] = x_ref[...] + g * o_ref[...]


def _ffn(x2, g, mods, base_shift, base_scale, base_gate, w1, w3, w2, l, B, ctx, bm, bf):
    M, D = x2.shape
    F = w1.shape[2]
    nf = F // bf
    nbb = (M // B) // bm
    return pl.pallas_call(
        functools.partial(_ffn_kernel, nb_per_batch=nbb, ctx=ctx, nf=nf),
        grid=(M // bm, nf),
        in_specs=[pl.BlockSpec((bm, D), lambda i, f: (i, 0)), pl.BlockSpec((1, D), lambda i, f: (0, 0))]
        + _mod_specs(base_shift, base_scale, nbb, B, D)
        + [
            pl.BlockSpec((1, D, bf), lambda i, f: (l, 0, f)),
            pl.BlockSpec((1, D, bf), lambda i, f: (l, 0, f)),
            pl.BlockSpec((1, bf, D), lambda i, f: (l, f, 0)),
            pl.BlockSpec((1, 1, D), lambda i, f: (base_gate + i // nbb, 0, 0)),
            pl.BlockSpec((1, 1, D), lambda i, f: (base_gate + B, 0, 0)),
        ],
        out_specs=pl.BlockSpec((bm, D), lambda i, f: (i, 0)),
        out_shape=jax.ShapeDtypeStruct((M, D), F32),
        scratch_shapes=[pltpu.VMEM((bm, D), BF16)],
        compiler_params=_cparams(("arbitrary", "arbitrary")),
    )(x2, g.reshape(1, D), mods, mods, mods, mods, w1, w3, w2, mods, mods)


def _moe_sparse_kernel(cnt_ref, h_ref, gate_ref, rank_ref, rankt_ref, w1_ref, w3_ref, w2_ref, o_ref,
                       hg_scr, acc_scr, *, nf, sub, first_slot, groups):
    i, e, f = pl.program_id(0), pl.program_id(1), pl.program_id(2)
    count = cnt_ref[i * N_EXPERTS + e]

    @pl.when((e == 0) & (f == 0))
    def _():
        o_ref[...] = jnp.zeros_like(o_ref)

    for g in range(groups):
        start = first_slot + g * sub
        active = count > start

        @pl.when(active & (f == 0))
        def _():
            slot = lax.broadcasted_iota(jnp.int32, (sub, h_ref.shape[0]), 0) + start
            pack = jnp.where(slot == rankt_ref[0, 0], 1.0, 0.0).astype(BF16)
            hg_scr[g] = _dot(pack, h_ref[...]).astype(BF16)
            acc_scr[g] = jnp.zeros((sub, h_ref.shape[1]), F32)

        @pl.when(active)
        def _():
            hg = hg_scr[g]
            hid = (_silu(_dot(hg, w1_ref[0, 0])) * _dot(hg, w3_ref[0, 0])).astype(BF16)
            acc_scr[g] += _dot(hid, w2_ref[0, 0])

        @pl.when(active & (f == nf - 1))
        def _():
            lane = lax.broadcasted_iota(jnp.int32, gate_ref.shape, 1)
            ge = jnp.sum(jnp.where(lane == e, gate_ref[...], 0.0), axis=-1, keepdims=True)
            rank = jnp.sum(jnp.where(lane == e, rank_ref[...], 0), axis=-1, keepdims=True)
            slot = lax.broadcasted_iota(jnp.int32, (h_ref.shape[0], sub), 1) + start
            unpack = jnp.where(slot == rank, 1.0, 0.0).astype(BF16)
            o_ref[...] += ge * _dot(unpack, acc_scr[g].astype(BF16))


def _moe_sparse(h, gates, w1, w3, w2, l, bm, bf):
    M, D = h.shape
    _, E, _, F = w1.shape
    nf, nblk = F // bf, M // bm
    sub = bm // 3
    routed = gates[:, :E].reshape(nblk, bm, E) > 0
    csum = jnp.cumsum(routed.astype(jnp.int32), axis=1)
    rank = jnp.where(routed, csum - 1, -1)
    counts = csum[:, -1, :].reshape(nblk * E)
    rank_lane = jnp.full((M, LANES), -1, jnp.int32).at[:, :E].set(rank.reshape(M, E))
    rank_t = rank.transpose(0, 2, 1).reshape(nblk, E, 1, bm)

    def run(first_slot, groups):
        return pl.pallas_call(
            functools.partial(_moe_sparse_kernel, nf=nf, sub=sub, first_slot=first_slot, groups=groups),
            grid_spec=pltpu.PrefetchScalarGridSpec(
                num_scalar_prefetch=1,
                grid=(nblk, E, nf),
                in_specs=[
                    pl.BlockSpec((bm, D), lambda i, e, f, c: (i, 0)),
                    pl.BlockSpec((bm, LANES), lambda i, e, f, c: (i, 0)),
                    pl.BlockSpec((bm, LANES), lambda i, e, f, c: (i, 0)),
                    pl.BlockSpec((1, 1, 1, bm), lambda i, e, f, c: (i, e, 0, 0)),
                    pl.BlockSpec((1, 1, D, bf), lambda i, e, f, c: (l, e, 0, f)),
                    pl.BlockSpec((1, 1, D, bf), lambda i, e, f, c: (l, e, 0, f)),
                    pl.BlockSpec((1, 1, bf, D), lambda i, e, f, c: (l, e, f, 0)),
                ],
                out_specs=pl.BlockSpec((bm, D), lambda i, e, f, c: (i, 0)),
                scratch_shapes=[pltpu.VMEM((groups, sub, D), BF16), pltpu.VMEM((groups, sub, D), F32)],
            ),
            out_shape=jax.ShapeDtypeStruct((M, D), F32),
            compiler_params=_cparams(("arbitrary", "arbitrary", "arbitrary")),
        )(counts, h, gates, rank_lane, rank_t, w1, w3, w2)

    delta = run(0, 2)
    return lax.cond(jnp.max(counts) > 2 * sub, lambda d: d + run(2 * sub, 1), lambda d: d, delta)


def _residual_kernel(x_ref, d_ref, g_ref, o_ref):
    o_ref[0] = x_ref[0] + g_ref[0] * d_ref[0]


def _gated_residual(xc, delta, mods, base_gate, ctx):
    B, TT, D = xc.shape
    bt = _pick(ctx, (256, 128, 64))
    blk = pl.BlockSpec((1, bt, D), lambda b, j: (b, j, 0))
    return pl.pallas_call(
        _residual_kernel, grid=(B, TT // bt),
        in_specs=[blk, blk, pl.BlockSpec((1, 1, D), functools.partial(_mod_index(base_gate, ctx // bt), nb=B))],
        out_specs=blk, out_shape=jax.ShapeDtypeStruct((B, TT, D), F32),
        compiler_params=_cparams(("arbitrary", "arbitrary")),
    )(xc, delta, mods)


def _rope_tables(seq, ctx, d):
    nf = d // 4
    rows = seq // GRID_W
    row = jnp.broadcast_to(jnp.arange(rows, dtype=jnp.int32)[:, None], (rows, GRID_W)).reshape(seq)
    col = jnp.broadcast_to(jnp.arange(GRID_W, dtype=jnp.int32)[None, :], (rows, GRID_W)).reshape(seq)
    inv_freq = ROPE_BASE ** (-jnp.arange(nf, dtype=F32) / nf)
    ar = row.astype(F32)[:, None] * inv_freq
    ac = col.astype(F32)[:, None] * inv_freq
    cos = jnp.concatenate([jnp.cos(ar), jnp.cos(ar), jnp.cos(ac), jnp.cos(ac)], axis=-1)
    sin = jnp.concatenate([-jnp.sin(ar), jnp.sin(ar), -jnp.sin(ac), jnp.sin(ac)], axis=-1)
    cos = jnp.concatenate([jnp.ones((ctx, d), F32), cos], axis=0)
    sin = jnp.concatenate([jnp.zeros((ctx, d), F32), sin], axis=0)
    return cos, sin


def _rope(x, cos, sin, quarter):
    lane = lax.broadcasted_iota(jnp.int32, x.shape, 1)
    up = pltpu.roll(x, LANES - quarter, 1)
    down = pltpu.roll(x, quarter, 1)
    swapped = jnp.where(lane % (2 * quarter) < quarter, up, down)
    return x * cos + swapped * sin


def _prep_kernel(qb_ref, kb_ref, vb_ref, qd0_ref, qd1_ref, kd0_ref, kd1_ref, vd0_ref, vd1_ref,
                 cb_ref, sb_ref, cc_ref, sc_ref, oqb, okb, ovb, oqd, okd, ovd):
    cb, sb, cc, sc = cb_ref[...], sb_ref[...], cc_ref[...], sc_ref[...]
    qb, kb = qb_ref[0], kb_ref[0]
    for h in range(HB):
        hs = slice(h * LANES, (h + 1) * LANES)
        oqb[0, :, hs] = _rope(qb[:, hs], cb, sb, DKB // 4).astype(BF16)
        okb[0, :, hs] = (_rope(kb[:, hs], cb, sb, DKB // 4) * DKB ** -0.5).astype(BF16)
    ovb[0] = vb_ref[0].astype(BF16)
    for half, (qr, kr, vr) in enumerate(((qd0_ref, kd0_ref, vd0_ref), (qd1_ref, kd1_ref, vd1_ref))):
        q, k, v = qr[0], kr[0], vr[0]
        for h in range(WA // LANES):
            hs = slice(h * LANES, (h + 1) * LANES)
            head = half * (WA // LANES) + h
            oqd[0, head] = (_rope(q[:, hs], cc, sc, DHC // 4) * Q_SCALE_C).T.astype(BF16)
            okd[0, :, half * WA + h * LANES:half * WA + (h + 1) * LANES] = _rope(k[:, hs], cc, sc, DHC // 4).astype(BF16)
            ovd[0, head] = v[:, hs].T.astype(BF16)


def _prep(p, tabs, ctx):
    B, TT, _ = p.shape
    bt = _pick(ctx, (256, 128, 64))
    pspec = lambda c: pl.BlockSpec((1, bt, WA), lambda b, j: (b, j, c))
    tspec = pl.BlockSpec((bt, LANES), lambda b, j: (j, 0))
    o512 = pl.BlockSpec((1, bt, WA), lambda b, j: (b, j, 0))
    o1024 = pl.BlockSpec((1, bt, 2 * WA), lambda b, j: (b, j, 0))
    o_t = pl.BlockSpec((1, HC, LANES, bt), lambda b, j: (b, 0, 0, j))
    s512 = jax.ShapeDtypeStruct((B, TT, WA), BF16)
    s1024 = jax.ShapeDtypeStruct((B, TT, 2 * WA), BF16)
    s_t = jax.ShapeDtypeStruct((B, HC, LANES, TT), BF16)
    cols = (COL_QB, COL_KB, COL_VB, COL_QD, COL_QD + 1, COL_KD, COL_KD + 1, COL_VD, COL_VD + 1)
    return pl.pallas_call(
        _prep_kernel, grid=(B, TT // bt),
        in_specs=[pspec(c) for c in cols] + [tspec] * 4,
        out_specs=[o512, o512, o512, o_t, o1024, o_t],
        out_shape=[s512, s512, s512, s_t, s1024, s_t],
        compiler_params=_cparams(("arbitrary", "arbitrary")),
    )(*([p] * 9), *tabs)


def _hgrn_consts():
    C = CHUNK_A
    t = np.arange(C)[:, None]
    u = np.arange(C)[None, :]
    out = []
    for rev in (False, True):
        mats = [(u >= t) if rev else (u <= t)]
        for m in LEVELS_A:
            r = (t // (2 * m)) * (2 * m) + (m if rev else m - 1)
            mats.append((u >= r) if rev else (u <= r))
        out.append(np.concatenate(mats, axis=0))
    return jnp.asarray(np.stack(out).astype(np.float32), dtype=BF16)


def _hgrn_kernel(qf_ref, zf_ref, vf_ref, qb_ref, zb_ref, vb_ref, cm_ref, lb_ref, of_ref, ob_ref, s_scr):
    C = CHUNK_A

    @pl.when(pl.program_id(1) == 0)
    def _():
        s_scr[...] = jnp.zeros_like(s_scr)

    row = lax.broadcasted_iota(jnp.int32, (C, LANES), 0)
    rr = lax.broadcasted_iota(jnp.int32, (C, C), 0)
    cc = lax.broadcasted_iota(jnp.int32, (C, C), 1)
    blk_row = lax.broadcasted_iota(jnp.int32, (C // SUBLANES, SUBLANES, LANES), 1)
    ones = jnp.ones((C, LANES), BF16)
    q_refs, z_refs, v_refs, o_refs = (qf_ref, qb_ref), (zf_ref, zb_ref), (vf_ref, vb_ref), (of_ref, ob_ref)

    ks, bbs, bend_ts = [], [], []
    for d in range(2):
        z = z_refs[d][0]
        llb, l1m = lb_ref[d, 0:1], lb_ref[d, 1:2]
        log_sig = jnp.minimum(z, 0.0) - jnp.log1p(jnp.exp(-jnp.abs(z)))
        t2 = l1m + log_sig
        logf = jnp.maximum(llb, t2) + jnp.log1p(jnp.exp(-jnp.abs(llb - t2)))
        ks.append(1.0 - jnp.exp(logf))
        parts = _split3(logf)
        cm = cm_ref[d]
        bbs.append(_dot(cm, parts[0]) + (_dot(cm, parts[1]) + _dot(cm, parts[2])))
        bend_ts.append(_dot_tn(parts[0], ones) + (_dot_tn(parts[1], ones) + _dot_tn(parts[2], ones)))

    heads = [(d, h) for d in range(2) for h in range(HA)]

    def operands(d, h):
        hs = slice(h * LANES, (h + 1) * LANES)
        return q_refs[d][0, :, hs], ks[d][:, hs], v_refs[d][0, :, hs], bbs[d][:C, hs]

    level_dots = {}
    for d, h in heads:
        q, k, _, b = operands(d, h)
        dots = []
        for lvl, m in enumerate(LEVELS_A):
            bref = bbs[d][(lvl + 1) * C:(lvl + 2) * C, h * LANES:(h + 1) * LANES]
            e = jnp.exp(-jnp.abs(b - bref))
            upper = (row % (2 * m)) >= m
            q_side = jnp.logical_not(upper) if d else upper
            qt = jnp.where(q_side, q * e, 0.0).astype(BF16)
            kt = jnp.where(q_side, 0.0, k * e).astype(BF16)
            dots.append(_dot_nt(qt, kt))
        level_dots[d, h] = dots

    diag = {}
    for d, h in heads:
        q3, k3, v3, b3 = (x.reshape(C // SUBLANES, SUBLANES, LANES) for x in operands(d, h))
        acc = jnp.zeros_like(q3)
        for j in range(SUBLANES):
            kj, vj, bj = k3[:, j:j + 1], v3[:, j:j + 1], b3[:, j:j + 1]
            e = jnp.exp(jnp.minimum(b3 - bj, 0.0))
            a = jnp.sum(q3 * kj * e, axis=-1, keepdims=True)
            valid = (blk_row <= j) if d else (blk_row >= j)
            acc = acc + jnp.where(valid, a, 0.0) * vj
        diag[d, h] = acc.reshape(C, LANES)

    outs, updates = {}, {}
    for d, h in heads:
        q, k, v, b = operands(d, h)
        v16 = v.astype(BF16)
        s_old = s_scr[d, h]
        att = jnp.zeros((C, C), F32)
        for m, a in zip(LEVELS_A, level_dots[d, h]):
            att = att + jnp.where((rr // (2 * m)) == (cc // (2 * m)), a, 0.0)
        inter = _dot((q * jnp.exp(b)).astype(BF16), s_old.astype(BF16))
        outs[d, h] = inter + _dot(att.astype(BF16), v16)
        end = 0 if d else C - 1
        bend = b[end:end + 1, :]
        kdec = (k * jnp.exp(bend - b)).astype(BF16)
        updates[d, h] = jnp.exp(bend_ts[d][h * LANES:(h + 1) * LANES, :]) * s_old + _dot_tn(kdec, v16)

    for d, h in heads:
        o_refs[d][0, :, h * LANES:(h + 1) * LANES] = outs[d, h] + diag[d, h]
        s_scr[d, h] = updates[d, h]


def _bwd_chunk(i, nctx, n):
    return jnp.where(i < nctx, nctx - 1 - i, n - 1 + nctx - i)


def _hgrn(p, lb_tab, ctx):
    B, TT, _ = p.shape
    C = CHUNK_A
    n, nctx = TT // C, ctx // C
    fwd = lambda c: pl.BlockSpec((1, C, WA), lambda b, i: (b, i, c))
    bwd = lambda c: pl.BlockSpec((1, C, WA), lambda b, i: (b, _bwd_chunk(i, nctx, n), c))
    cm = _hgrn_consts()
    shape = jax.ShapeDtypeStruct((B, TT, WA), F32)
    return pl.pallas_call(
        _hgrn_kernel, grid=(B, n),
        in_specs=[fwd(COL_QA), fwd(COL_FF), fwd(COL_IA), bwd(COL_QA), bwd(COL_FB), bwd(COL_IA),
                  pl.BlockSpec(cm.shape, lambda b, i: (0, 0, 0)),
                  pl.BlockSpec((2, 2, WA), lambda b, i: (0, 0, 0))],
        out_specs=[pl.BlockSpec((1, C, WA), lambda b, i: (b, i, 0)),
                   pl.BlockSpec((1, C, WA), lambda b, i: (b, _bwd_chunk(i, nctx, n), 0))],
        out_shape=[shape, shape],
        scratch_shapes=[pltpu.VMEM((2, HA, DKA, DVA), F32)],
        compiler_params=_cparams(("arbitrary", "arbitrary")),
    )(p, p, p, p, p, p, cm, lb_tab)


def _ret_kernel(lg_ref, qf_ref, kf_ref, vf_ref, qb_ref, kb_ref, vb_ref, of_ref, ob_ref,
                s_scr, d_scr, rq_scr, rk_scr, gc_scr):
    C = CHUNK_B

    @pl.when((pl.program_id(0) == 0) & (pl.program_id(1) == 0))
    def _():
        t = lax.broadcasted_iota(jnp.int32, (C, C), 0)
        s = lax.broadcasted_iota(jnp.int32, (C, C), 1)
        pos = lax.broadcasted_iota(jnp.int32, (C, LANES), 0).astype(F32)
        for d in range(2):
            for h in range(HB):
                lg = lg_ref[d, h]
                delta = (s - t) if d else (t - s)
                d_scr[d, h] = jnp.where(delta >= 0, jnp.exp(jnp.maximum(delta, 0).astype(F32) * lg), 0.0)
                rq_scr[d, h] = jnp.exp(((C - pos) if d else (pos + 1.0)) * lg)
                rk_scr[d, h] = jnp.exp((pos if d else (C - 1.0 - pos)) * lg)
                gc_scr[d, h] = jnp.exp(jnp.full((SUBLANES, LANES), C, F32) * lg)

    @pl.when(pl.program_id(1) == 0)
    def _():
        s_scr[...] = jnp.zeros_like(s_scr)

    for d, (q_ref, k_ref, v_ref, o_ref) in enumerate(((qf_ref, kf_ref, vf_ref, of_ref), (qb_ref, kb_ref, vb_ref, ob_ref))):
        for h in range(HB):
            hs = slice(h * LANES, (h + 1) * LANES)
            q, k, v = q_ref[0, :, hs], k_ref[0, :, hs], v_ref[0, :, hs]
            att = (_dot_nt(q, k) * d_scr[d, h]).astype(BF16)
            s_old = s_scr[d, h]
            o_ref[0, :, hs] = _dot(att, v) + rq_scr[d, h] * _dot(q, s_old.astype(BF16))
            kdec = (k.astype(F32) * rk_scr[d, h]).astype(BF16)
            s_scr[d, h] = gc_scr[d, h][0:1, :] * s_old + _dot_tn(kdec, v)


def _retention(qb, kb, vb, log_gamma, ctx):
    B, TT, _ = qb.shape
    C = CHUNK_B
    n, nctx = TT // C, ctx // C
    fwd = pl.BlockSpec((1, C, WA), lambda b, i: (b, i, 0))
    bwd = pl.BlockSpec((1, C, WA), lambda b, i: (b, _bwd_chunk(i, nctx, n), 0))
    shape = jax.ShapeDtypeStruct((B, TT, WA), F32)
    return pl.pallas_call(
        _ret_kernel, grid=(B, n),
        in_specs=[pl.BlockSpec(memory_space=pltpu.SMEM), fwd, fwd, fwd, bwd, bwd, bwd],
        out_specs=[fwd, bwd], out_shape=[shape, shape],
        scratch_shapes=[pltpu.VMEM((2, HB, DKB, DVB), F32), pltpu.VMEM((2, HB, C, C), F32),
                        pltpu.VMEM((2, HB, C, LANES), F32), pltpu.VMEM((2, HB, C, LANES), F32),
                        pltpu.VMEM((2, HB, SUBLANES, LANES), F32)],
        compiler_params=_cparams(("arbitrary", "arbitrary")),
    )(log_gamma, qb, kb, vb, qb, kb, vb)


def _attn_kernel(lam_ref, q_ref, k_ref, v_ref, o_ref, qm_scr, m_scr, l_scr, acc_scr, *, ctx, tq, tk, nk):
    qi = pl.program_id(2)
    ki = pl.program_id(3)

    @pl.when(ki == 0)
    def _():
        q = q_ref[0, 0]
        row = lax.broadcasted_iota(jnp.int32, q.shape, 0)
        zero = jnp.zeros_like(q)
        qm_scr[0] = jnp.where(row < DHC, q, zero)
        qm_scr[1] = jnp.where(row < DHC, zero, q)
        m_scr[...] = jnp.full_like(m_scr, NEG_BIG)
        l_scr[...] = jnp.zeros_like(l_scr)
        acc_scr[...] = jnp.zeros_like(acc_scr)

    def step(masked):
        k = k_ref[0]
        vt = v_ref[0, 0]
        m_old, l_old, acc_old = m_scr[...], l_scr[...], acc_scr[...]
        key_subs = [slice(j, j + K_SUB) for j in range(0, tk, K_SUB)]
        m_out, l_out, acc_out = [[], []], [[], []], [[], []]
        chains = [(g, c) for g in range(tq // Q_GROUP) for c in range(2)]

        def scores(g, c):
            qg = qm_scr[c, :, g * Q_GROUP:(g + 1) * Q_GROUP]
            s = [_dot(k[ks], qg) for ks in key_subs]
            if masked and g * Q_GROUP < ctx:
                keys = lax.broadcasted_iota(jnp.int32, (tk, Q_GROUP), 0) + ki * tk
                qpos = lax.broadcasted_iota(jnp.int32, (tk, Q_GROUP), 1) + g * Q_GROUP
                hidden = (qpos < ctx) & (keys >= ctx)
                s = [jnp.where(hidden[ks], NEG_BIG, sj) for ks, sj in zip(key_subs, s)]
            return s

        def finish(g, c, s):
            gs = slice(g * Q_GROUP, (g + 1) * Q_GROUP)
            m_prev = m_old[c, :, gs]
            m_new = functools.reduce(jnp.maximum, [jnp.max(sj, axis=0, keepdims=True) for sj in s], m_prev)
            alpha = jnp.exp2(m_prev - m_new)
            p = [jnp.exp2(sj - m_new) for sj in s]
            l_new = functools.reduce(jnp.add, [jnp.sum(pj, axis=0, keepdims=True) for pj in p])
            pv = functools.reduce(jnp.add, [_dot(vt[:, ks], pj.astype(BF16)) for ks, pj in zip(key_subs, p)])
            l_out[c].append(alpha * l_old[c, :, gs] + l_new)
            acc_out[c].append(alpha * acc_old[c, :, gs] + pv)
            m_out[c].append(m_new)

        pending = [scores(*ch) for ch in chains[:SCORE_LOOKAHEAD]]
        for n, chain in enumerate(chains):
            if n + SCORE_LOOKAHEAD < len(chains):
                pending.append(scores(*chains[n + SCORE_LOOKAHEAD]))
            finish(*chain, pending.pop(0))
        for c in range(2):
            m_scr[c] = jnp.concatenate(m_out[c], axis=-1)
            l_scr[c] = jnp.concatenate(l_out[c], axis=-1)
            acc_scr[c] = jnp.concatenate(acc_out[c], axis=-1)

    @pl.when(qi == 0)
    def _():
        step(True)

    @pl.when(qi > 0)
    def _():
        step(False)

    @pl.when(ki == nk - 1)
    def _():
        o_t = acc_scr[0] / l_scr[0] - lam_ref[0] * (acc_scr[1] / l_scr[1])
        o_ref[0] = o_t.T


def _diff_attention(qd, kd, vd, lam, ctx):
    B, TT, _ = kd.shape
    tq = _pick(TT, (2816, 768, 512, 256))
    tk = _pick(TT, (768, 512, 256))
    assert ctx <= tk and ctx <= tq and tq % Q_GROUP == 0 and tk % K_SUB == 0
    nq, nk = TT // tq, TT // tk
    return pl.pallas_call(
        functools.partial(_attn_kernel, ctx=ctx, tq=tq, tk=tk, nk=nk),
        grid=(B, HC, nq, nk),
        in_specs=[pl.BlockSpec(memory_space=pltpu.SMEM),
                  pl.BlockSpec((1, 1, LANES, tq), lambda b, h, i, j: (b, h, 0, i)),
                  pl.BlockSpec((1, tk, LANES), lambda b, h, i, j: (b, j, h)),
                  pl.BlockSpec((1, 1, LANES, tk), lambda b, h, i, j: (b, h, 0, j))],
        out_specs=pl.BlockSpec((1, tq, LANES), lambda b, h, i, j: (b, i, h)),
        out_shape=jax.ShapeDtypeStruct((B, TT, HC * DVC), F32),
        scratch_shapes=[pltpu.VMEM((2, LANES, tq), BF16), pltpu.VMEM((2, 1, tq), F32),
                        pltpu.VMEM((2, 1, tq), F32), pltpu.VMEM((2, DVC, tq), F32)],
        compiler_params=_cparams(("arbitrary", "arbitrary", "arbitrary", "arbitrary")),
    )(lam, qd, kd, vd)


def _merge_kernel(af_ref, ab_ref, bf_ref, bb_ref, c_ref, ga_ref, gb_ref, na_ref, nb_ref, nc_ref, o_ref, *, c_scale):
    oa = af_ref[0] + ab_ref[0]
    ob = bf_ref[0] + bb_ref[0]
    oc = c_ref[0]
    ga, gb = ga_ref[0], gb_ref[0]
    for h in range(HA):
        hs = slice(h * LANES, (h + 1) * LANES)
        o_ref[0, :, hs] = (_rms(oa[:, hs]) * na_ref[:, hs] * _silu(ga[:, hs])).astype(o_ref.dtype)
    for h in range(HB):
        hs = slice(h * LANES, (h + 1) * LANES)
        os_ = slice(WA + h * LANES, WA + (h + 1) * LANES)
        o_ref[0, :, os_] = (_rms(ob[:, hs]) * nb_ref[:, hs] * _silu(gb[:, hs])).astype(o_ref.dtype)
    for h in range(HC):
        hs = slice(h * LANES, (h + 1) * LANES)
        os_ = slice(2 * WA + h * LANES, 2 * WA + (h + 1) * LANES)
        o_ref[0, :, os_] = (_rms(oc[:, hs]) * nc_ref[:, hs] * c_scale).astype(o_ref.dtype)


def _merge(oa_f, oa_b, ob_f, ob_b, oc, p, g_a, g_b, g_c, lam_init, ctx):
    B, TT, _ = oc.shape
    bt = _pick(ctx, (256, 128, 64))
    s512 = pl.BlockSpec((1, bt, WA), lambda b, j: (b, j, 0))
    pcol = lambda c: pl.BlockSpec((1, bt, WA), lambda b, j: (b, j, c))
    return pl.pallas_call(
        functools.partial(_merge_kernel, c_scale=1.0 - lam_init),
        grid=(B, TT // bt),
        in_specs=[s512, s512, s512, s512, pl.BlockSpec((1, bt, 2 * WA), lambda b, j: (b, j, 0)),
                  pcol(COL_GA), pcol(COL_GB),
                  pl.BlockSpec((1, WA), lambda b, j: (0, 0)), pl.BlockSpec((1, WA), lambda b, j: (0, 0)),
                  pl.BlockSpec((1, 2 * WA), lambda b, j: (0, 0))],
        out_specs=pl.BlockSpec((1, bt, D_MIX), lambda b, j: (b, j, 0)),
        out_shape=jax.ShapeDtypeStruct((B, TT, D_MIX), BF16),
        compiler_params=_cparams(("arbitrary", "arbitrary")),
    )(oa_f, oa_b, ob_f, ob_b, oc, p, p, g_a.reshape(1, -1), g_b.reshape(1, -1), g_c.reshape(1, -1))


def kernel(x, c, ctx, c_ctx, w_ada, b_ada, norm_mix, w_in, hgrn_lower_bounds, ret_decay, diff_lambda, norm_a, norm_b, norm_c, w_out, norm_ffn, ffn_w1, ffn_w3, ffn_w2, router, moe_w1, moe_w3, moe_w2, final_norm):
    B, S, D = x.shape
    CTX = ctx.shape[1]
    L = w_ada.shape[0]
    TT = CTX + S
    M = B * TT
    F = ffn_w1.shape[-1]
    assert w_in.shape[-1] == D_PROJ and w_out.shape[1] == D_MIX
    assert B + 1 <= SUBLANES and CTX % CHUNK_B == 0 and S % CHUNK_B == 0 and S % GRID_W == 0

    lbs = jnp.cumsum(jax.nn.softmax(hgrn_lower_bounds.astype(F32), axis=0), axis=0)
    lbs = lbs - lbs[0:1]
    lb_tabs = jnp.stack([jnp.log(lbs), jnp.log1p(-lbs)], axis=2)
    log_gammas = jnp.log1p(-jnp.exp2(-ret_decay.astype(F32)))
    tabs = _rope_tables(S, CTX, DKB) + _rope_tables(S, CTX, DHC)
    tabs = tabs[:2] + tuple(jnp.concatenate([t, t], axis=-1) for t in tabs[2:])

    cvec = jnp.zeros((SUBLANES, D), F32).at[:B].set(c).at[B].set(c_ctx)
    mods = _ada_all(cvec, w_ada, b_ada)
    mods = mods.reshape(L, SUBLANES, 6, D).transpose(0, 2, 1, 3).reshape(L * 6 * SUBLANES, 1, D)
    mod_base = lambda l, chunk: (l * 6 + chunk) * SUBLANES

    bm = _pick(TT, (768, 384, 256, 128))
    bn_in = _pick(D_PROJ, (1280, 768, 512))
    bn_out = _pick(D, (1024, 512, 256))
    bf = _pick(F, (512, 256, 128))
    bm_moe = _pick(M, (1056, 768, 384))
    bf_moe = _pick(F, (512, 256, 128))

    w_in16, w_out16 = w_in.astype(BF16), w_out.astype(BF16)
    ffn16 = tuple(w.astype(BF16) for w in (ffn_w1, ffn_w3, ffn_w2))
    moe16 = tuple(w.astype(BF16) for w in (moe_w1, moe_w3, moe_w2))

    xc = jnp.concatenate([ctx, x], axis=1)
    for l in range(L):
        lam_init = 0.8 - 0.6 * math.exp(-0.3 * l)
        lq1, lk1, lq2, lk2 = diff_lambda[l].astype(F32)
        lam = (jnp.exp(jnp.sum(lq1 * lk1)) - jnp.exp(jnp.sum(lq2 * lk2)) + lam_init).reshape(1)

        p = _modulated_matmul(xc.reshape(M, D), norm_mix[l], mods, mod_base(l, 0), mod_base(l, 1), w_in16, l,
                              B, CTX, bm, bn_in).reshape(B, TT, D_PROJ)
        oa_f, oa_b = _hgrn(p, lb_tabs[l], CTX)
        qb, kb, vb, qd, kd, vd = _prep(p, tabs, CTX)
        ob_f, ob_b = _retention(qb, kb, vb, log_gammas[l], CTX)
        oc = _diff_attention(qd, kd, vd, lam, CTX)
        y = _merge(oa_f, oa_b, ob_f, ob_b, oc, p, norm_a[l], norm_b[l], norm_c[l], lam_init, CTX)
        xc = _matmul_residual(y.reshape(M, D_MIX), w_out16, l, xc.reshape(M, D), mods,
                              mod_base(l, 2), B, CTX, bm, bn_out).reshape(B, TT, D)

        if l % 2 == 0:
            xc = _ffn(xc.reshape(M, D), norm_ffn[l], mods, mod_base(l, 3), mod_base(l, 4), mod_base(l, 5), *ffn16,
                      l // 2, B, CTX, bm, bf).reshape(B, TT, D)
        else:
            e = l // 2
            wr = jnp.zeros((D, LANES), F32).at[:, :N_EXPERTS].set(router[e])
            wr_hi = wr.astype(BF16)
            wr2 = jnp.stack([wr_hi, (wr - wr_hi.astype(F32)).astype(BF16)])
            h, gates = _modulate(xc, norm_ffn[l], mods, mod_base(l, 3), mod_base(l, 4), CTX, router=wr2)
            delta = _moe_sparse(h.reshape(M, D), gates.reshape(M, LANES), *moe16, e, bm_moe, bf_moe)
            xc = _gated_residual(xc, delta.reshape(B, TT, D), mods, mod_base(l, 5), CTX)

    zeros = jnp.zeros((SUBLANES, 1, D), F32)
    return _modulate(xc, final_norm, zeros, 0, 0, 0, out_dtype=F32, row_offset=CTX, rows=S)
```

```python
import functools
import math

import numpy as np
import jax
import jax.numpy as jnp
from jax import lax
from jax.experimental import pallas as pl
from jax.experimental.pallas import tpu as pltpu

F32 = jnp.float32
BF16 = jnp.bfloat16

HA, DKA, DVA = 4, 128, 128
HB, DKB, DVB = 4, 128, 128
HC, DHC, DVC = 8, 64, 128
GRID_W = 64
N_EXPERTS = 8
ROPE_BASE = 10000.0
EPS = 1e-6
WA = HA * DKA
D_PROJ = 5 * WA + 4 * WA + 3 * HC * DVC
D_MIX = HA * DVA + HB * DVB + HC * DVC
COL_QA, COL_FF, COL_FB, COL_IA, COL_GA, COL_QB, COL_KB, COL_VB, COL_GB = range(9)
COL_QD, COL_KD, COL_VD = 9, 11, 13

LANES = 128
SUBLANES = 8
VMEM_LIMIT_MB = 56

CHUNK_A = 64
LEVELS_A = (32, 16, 8)
CHUNK_B = 256
NEG_BIG = -1e30
Q_SCALE_C = DHC ** -0.5 * math.log2(math.e)
Q_GROUP = 256
K_SUB = 256
SCORE_LOOKAHEAD = 3


def _cparams(semantics, vmem_mb=VMEM_LIMIT_MB):
    return pltpu.CompilerParams(dimension_semantics=semantics, vmem_limit_bytes=vmem_mb * 2 ** 20)


def _pick(n, candidates):
    for c in candidates:
        if n % c == 0:
            return c
    raise ValueError(f"no block size for {n} in {candidates}")


def _split3(x):
    hi = x.astype(BF16)
    r1 = x - hi.astype(F32)
    mid = r1.astype(BF16)
    lo = (r1 - mid.astype(F32)).astype(BF16)
    return hi, mid, lo


def _dot(a, b):
    return jnp.dot(a, b, preferred_element_type=F32)


def _dot_nt(a, b):
    return lax.dot_general(a, b, (((1,), (1,)), ((), ())), preferred_element_type=F32)


def _dot_tn(a, b):
    return lax.dot_general(a, b, (((0,), (0,)), ((), ())), preferred_element_type=F32)


def _ada_kernel(c_ref, w_ref, b_ref, o_ref):
    c = c_ref[...]
    s = c * jax.nn.sigmoid(c)
    o_ref[0] = _dot(s.astype(BF16), w_ref[0].astype(BF16)) + b_ref[0]


def _ada_all(cvec, w_ada, b_ada):
    L, D, N = w_ada.shape
    bn = _pick(N, (1024, 768, 512, 256, 128))
    return pl.pallas_call(
        _ada_kernel,
        grid=(L, N // bn),
        in_specs=[
            pl.BlockSpec((SUBLANES, D), lambda l, j: (0, 0)),
            pl.BlockSpec((1, D, bn), lambda l, j: (l, 0, j)),
            pl.BlockSpec((1, 1, bn), lambda l, j: (l, 0, j)),
        ],
        out_specs=pl.BlockSpec((1, SUBLANES, bn), lambda l, j: (l, 0, j)),
        out_shape=jax.ShapeDtypeStruct((L, SUBLANES, N), F32),
        compiler_params=_cparams(("arbitrary", "arbitrary")),
    )(cvec, w_ada, b_ada.reshape(L, 1, N))


def _rms(x):
    return x * lax.rsqrt(jnp.mean(x * x, axis=-1, keepdims=True) + EPS)


def _modulate_kernel(x_ref, g_ref, sh_ref, sc_ref, o_ref):
    y = _rms(x_ref[0]) * g_ref[...]
    o_ref[0] = (y * (1 + sc_ref[0]) + sh_ref[0]).astype(o_ref.dtype)


def _modulate_route_kernel(x_ref, g_ref, sh_ref, sc_ref, wr_ref, o_ref, gate_ref):
    y = _rms(x_ref[0]) * g_ref[...]
    h = y * (1 + sc_ref[0]) + sh_ref[0]
    o_ref[0] = h.astype(o_ref.dtype)
    hi = h.astype(BF16)
    lo = (h - hi.astype(F32)).astype(BF16)
    logits = _dot(hi, wr_ref[0]) + (_dot(lo, wr_ref[0]) + _dot(hi, wr_ref[1]))
    lane = lax.broadcasted_iota(jnp.int32, logits.shape, 1)
    logits = jnp.where(lane < N_EXPERTS, logits, -jnp.inf)
    m1 = jnp.max(logits, axis=-1, keepdims=True)
    i1 = jnp.min(jnp.where(logits == m1, lane, LANES), axis=-1, keepdims=True)
    rest = jnp.where(lane == i1, -jnp.inf, logits)
    m2 = jnp.max(rest, axis=-1, keepdims=True)
    i2 = jnp.min(jnp.where(rest == m2, lane, LANES), axis=-1, keepdims=True)
    e2 = jnp.exp(m2 - m1)
    w1 = 1.0 / (1.0 + e2)
    gate_ref[0] = jnp.where(lane == i1, w1, 0.0) + jnp.where(lane == i2, e2 * w1, 0.0)


def _mod_index(base, nctx_blocks):
    def index(b, j, *, nb):
        return (base + jnp.where(j < nctx_blocks, nb, b), 0, 0)
    return index


def _modulate(xc, g, mods, base_shift, base_scale, ctx, out_dtype=BF16, router=None, row_offset=0, rows=None):
    B, TT, D = xc.shape
    bt = _pick(ctx, (256, 128, 64))
    rows = TT if rows is None else rows
    off = row_offset // bt
    nctx = max(ctx // bt - off, 0)
    sh_idx = functools.partial(_mod_index(base_shift, nctx), nb=B)
    sc_idx = functools.partial(_mod_index(base_scale, nctx), nb=B)
    in_specs = [
        pl.BlockSpec((1, bt, D), lambda b, j: (b, j + off, 0)),
        pl.BlockSpec((1, D), lambda b, j: (0, 0)),
        pl.BlockSpec((1, 1, D), sh_idx),
        pl.BlockSpec((1, 1, D), sc_idx),
    ]
    out_spec = pl.BlockSpec((1, bt, D), lambda b, j: (b, j, 0))
    out_shape = jax.ShapeDtypeStruct((B, rows, D), out_dtype)
    if router is None:
        return pl.pallas_call(
            _modulate_kernel, grid=(B, rows // bt), in_specs=in_specs, out_specs=out_spec, out_shape=out_shape,
            compiler_params=_cparams(("arbitrary", "arbitrary")),
        )(xc, g.reshape(1, D), mods, mods)
    in_specs.append(pl.BlockSpec((2, D, LANES), lambda b, j: (0, 0, 0)))
    return pl.pallas_call(
        _modulate_route_kernel, grid=(B, rows // bt), in_specs=in_specs,
        out_specs=[out_spec, pl.BlockSpec((1, bt, LANES), lambda b, j: (b, j, 0))],
        out_shape=[out_shape, jax.ShapeDtypeStruct((B, rows, LANES), F32)],
        compiler_params=_cparams(("arbitrary", "arbitrary")),
    )(xc, g.reshape(1, D), mods, mods, router)


def _row_gate(shape, i, nb_per_batch, ctx, g_ctx, g_lat):
    nctx = jnp.where(i % nb_per_batch == 0, ctx, 0)
    rows = lax.broadcasted_iota(jnp.int32, shape, 0)
    return jnp.where(rows < nctx, g_ctx, g_lat)


def _modulated(x, g, shl_ref, shc_ref, scl_ref, scc_ref, i, nb_per_batch, ctx):
    shift = _row_gate(x.shape, i, nb_per_batch, ctx, shc_ref[0], shl_ref[0])
    scale = _row_gate(x.shape, i, nb_per_batch, ctx, scc_ref[0], scl_ref[0])
    return _rms(x) * g * (1 + scale) + shift


def _mod_specs(base_shift, base_scale, nbb, B, D):
    def spec(base, ctx_row):
        return pl.BlockSpec((1, 1, D), lambda i, j: (base + (B if ctx_row else i // nbb), 0, 0))
    return [spec(base_shift, False), spec(base_shift, True), spec(base_scale, False), spec(base_scale, True)]


def _mm_mod_kernel(x_ref, g_ref, shl_ref, shc_ref, scl_ref, scc_ref, w_ref, o_ref, h_scr, *, nb_per_batch, ctx):
    @pl.when(pl.program_id(1) == 0)
    def _():
        h = _modulated(x_ref[...], g_ref[...], shl_ref, shc_ref, scl_ref, scc_ref, pl.program_id(0), nb_per_batch, ctx)
        h_scr[...] = h.astype(BF16)

    o_ref[...] = _dot(h_scr[...], w_ref[0]).astype(o_ref.dtype)


def _modulated_matmul(x2, g, mods, base_shift, base_scale, w, l, B, ctx, bm, bn):
    M, K = x2.shape
    N = w.shape[2]
    nbb = (M // B) // bm
    return pl.pallas_call(
        functools.partial(_mm_mod_kernel, nb_per_batch=nbb, ctx=ctx), grid=(M // bm, N // bn),
        in_specs=[pl.BlockSpec((bm, K), lambda i, j: (i, 0)), pl.BlockSpec((1, K), lambda i, j: (0, 0))]
        + _mod_specs(base_shift, base_scale, nbb, B, K)
        + [pl.BlockSpec((1, K, bn), lambda i, j: (l, 0, j))],
        out_specs=pl.BlockSpec((bm, bn), lambda i, j: (i, j)),
        out_shape=jax.ShapeDtypeStruct((M, N), F32),
        scratch_shapes=[pltpu.VMEM((bm, K), BF16)],
        compiler_params=_cparams(("arbitrary", "arbitrary")),
    )(x2, g.reshape(1, K), mods, mods, mods, mods, w)


def _mm_res_kernel(a_ref, w_ref, x_ref, gl_ref, gc_ref, o_ref, *, nb_per_batch, ctx):
    acc = _dot(a_ref[...], w_ref[0])
    g = _row_gate(acc.shape, pl.program_id(0), nb_per_batch, ctx, gc_ref[0], gl_ref[0])
    o_ref[...] = x_ref[...] + g * acc


def _matmul_residual(a, w, l, x2, mods, base_gate, B, ctx, bm, bn):
    M, K = a.shape
    N = w.shape[2]
    nbb = (M // B) // bm
    return pl.pallas_call(
        functools.partial(_mm_res_kernel, nb_per_batch=nbb, ctx=ctx),
        grid=(M // bm, N // bn),
        in_specs=[
            pl.BlockSpec((bm, K), lambda i, j: (i, 0)),
            pl.BlockSpec((1, K, bn), lambda i, j: (l, 0, j)),
            pl.BlockSpec((bm, bn), lambda i, j: (i, j)),
            pl.BlockSpec((1, 1, bn), lambda i, j: (base_gate + i // nbb, 0, j)),
            pl.BlockSpec((1, 1, bn), lambda i, j: (base_gate + B, 0, j)),
        ],
        out_specs=pl.BlockSpec((bm, bn), lambda i, j: (i, j)),
        out_shape=jax.ShapeDtypeStruct((M, N), F32),
        compiler_params=_cparams(("arbitrary", "arbitrary")),
    )(a, w, x2, mods, mods)


def _silu(x):
    return x * jax.nn.sigmoid(x)


def _ffn_kernel(x_ref, g_ref, shl_ref, shc_ref, scl_ref, scc_ref, w1_ref, w3_ref, w2_ref, gl_ref, gc_ref, o_ref,
                h_scr, *, nb_per_batch, ctx, nf):
    f = pl.program_id(1)

    @pl.when(f == 0)
    def _():
        h = _modulated(x_ref[...], g_ref[...], shl_ref, shc_ref, scl_ref, scc_ref, pl.program_id(0), nb_per_batch, ctx)
        h_scr[...] = h.astype(BF16)
        o_ref[...] = jnp.zeros_like(o_ref)

    h = h_scr[...]
    hid = (_silu(_dot(h, w1_ref[0])) * _dot(h, w3_ref[0])).astype(BF16)
    o_ref[...] += _dot(hid, w2_ref[0])

    @pl.when(f == nf - 1)
    def _():
        g = _row_gate(o_ref.shape, pl.program_id(0), nb_per_batch, ctx, gc_ref[0], gl_ref[0])
        o_ref[...] = x_ref[...] + g * o_ref[...]


def _ffn(x2, g, mods, base_shift, base_scale, base_gate, w1, w3, w2, l, B, ctx, bm, bf):
    M, D = x2.shape
    F = w1.shape[2]
    nf = F // bf
    nbb = (M // B) // bm
    return pl.pallas_call(
        functools.partial(_ffn_kernel, nb_per_batch=nbb, ctx=ctx, nf=nf),
        grid=(M // bm, nf),
        in_specs=[pl.BlockSpec((bm, D), lambda i, f: (i, 0)), pl.BlockSpec((1, D), lambda i, f: (0, 0))]
        + _mod_specs(base_shift, base_scale, nbb, B, D)
        + [
            pl.BlockSpec((1, D, bf), lambda i, f: (l, 0, f)),
            pl.BlockSpec((1, D, bf), lambda i, f: (l, 0, f)),
            pl.BlockSpec((1, bf, D), lambda i, f: (l, f, 0)),
            pl.BlockSpec((1, 1, D), lambda i, f: (base_gate + i // nbb, 0, 0)),
            pl.BlockSpec((1, 1, D), lambda i, f: (base_gate + B, 0, 0)),
        ],
        out_specs=pl.BlockSpec((bm, D), lambda i, f: (i, 0)),
        out_shape=jax.ShapeDtypeStruct((M, D), F32),
        scratch_shapes=[pltpu.VMEM((bm, D), BF16)],
        compiler_params=_cparams(("arbitrary", "arbitrary")),
    )(x2, g.reshape(1, D), mods, mods, mods, mods, w1, w3, w2, mods, mods)


def _moe_sparse_kernel(cnt_ref, h_ref, gate_ref, rank_ref, rankt_ref, w1_ref, w3_ref, w2_ref, o_ref,
                       hg_scr, acc_scr, *, nf, unit, first_slot, classes):
    i, e, f = pl.program_id(0), pl.program_id(1), pl.program_id(2)
    left = cnt_ref[i * N_EXPERTS + e] - first_slot

    @pl.when((e == 0) & (f == 0))
    def _():
        o_ref[...] = jnp.zeros_like(o_ref)

    for c in range(1, classes + 1):
        m = c * unit
        fits = (left > m - unit) if c == classes else ((left > m - unit) & (left <= m))

        @pl.when(fits & (f == 0))
        def _():
            slot = lax.broadcasted_iota(jnp.int32, (m, h_ref.shape[0]), 0) + first_slot
            pack = jnp.where(slot == rankt_ref[0, 0], 1.0, 0.0).astype(BF16)
            hg_scr[pl.ds(0, m)] = _dot(pack, h_ref[...]).astype(BF16)
            acc_scr[pl.ds(0, m)] = jnp.zeros((m, h_ref.shape[1]), F32)

        @pl.when(fits)
        def _():
            hg = hg_scr[pl.ds(0, m)]
            hid = (_silu(_dot(hg, w1_ref[0, 0])) * _dot(hg, w3_ref[0, 0])).astype(BF16)
            acc_scr[pl.ds(0, m)] += _dot(hid, w2_ref[0, 0])

        @pl.when(fits & (f == nf - 1))
        def _():
            lane = lax.broadcasted_iota(jnp.int32, gate_ref.shape, 1)
            ge = jnp.sum(jnp.where(lane == e, gate_ref[...], 0.0), axis=-1, keepdims=True)
            rank = jnp.sum(jnp.where(lane == e, rank_ref[...], 0), axis=-1, keepdims=True)
            slot = lax.broadcasted_iota(jnp.int32, (h_ref.shape[0], m), 1) + first_slot
            unpack = jnp.where(slot == rank, 1.0, 0.0).astype(BF16)
            o_ref[...] += ge * _dot(unpack, acc_scr[pl.ds(0, m)].astype(BF16))


def _moe_sparse(h, gates, w1, w3, w2, l, bm, bf):
    M, D = h.shape
    _, E, _, F = w1.shape
    nf, nblk = F // bf, M // bm
    unit = bm // 6
    assert unit % 16 == 0 and 6 * unit == bm
    routed = gates[:, :E].reshape(nblk, bm, E) > 0
    csum = jnp.cumsum(routed.astype(jnp.int32), axis=1)
    rank = jnp.where(routed, csum - 1, -1)
    counts = csum[:, -1, :].reshape(nblk * E)
    rank_lane = jnp.full((M, LANES), -1, jnp.int32).at[:, :E].set(rank.reshape(M, E))
    rank_t = rank.transpose(0, 2, 1).reshape(nblk, E, 1, bm)

    def run(first_slot, classes):
        return pl.pallas_call(
            functools.partial(_moe_sparse_kernel, nf=nf, unit=unit, first_slot=first_slot, classes=classes),
            grid_spec=pltpu.PrefetchScalarGridSpec(
                num_scalar_prefetch=1,
                grid=(nblk, E, nf),
                in_specs=[
                    pl.BlockSpec((bm, D), lambda i, e, f, c: (i, 0)),
                    pl.BlockSpec((bm, LANES), lambda i, e, f, c: (i, 0)),
                    pl.BlockSpec((bm, LANES), lambda i, e, f, c: (i, 0)),
                    pl.BlockSpec((1, 1, 1, bm), lambda i, e, f, c: (i, e, 0, 0)),
                    pl.BlockSpec((1, 1, D, bf), lambda i, e, f, c: (l, e, 0, f)),
                    pl.BlockSpec((1, 1, D, bf), lambda i, e, f, c: (l, e, 0, f)),
                    pl.BlockSpec((1, 1, bf, D), lambda i, e, f, c: (l, e, f, 0)),
                ],
                out_specs=pl.BlockSpec((bm, D), lambda i, e, f, c: (i, 0)),
                scratch_shapes=[pltpu.VMEM((classes * unit, D), BF16), pltpu.VMEM((classes * unit, D), F32)],
            ),
            out_shape=jax.ShapeDtypeStruct((M, D), F32),
            compiler_params=_cparams(("arbitrary", "arbitrary", "arbitrary")),
        )(counts, h, gates, rank_lane, rank_t, w1, w3, w2)

    delta = run(0, 4)
    return lax.cond(jnp.max(counts) > 4 * unit, lambda d: d + run(4 * unit, 2), lambda d: d, delta)


def _residual_kernel(x_ref, d_ref, g_ref, o_ref):
    o_ref[0] = x_ref[0] + g_ref[0] * d_ref[0]


def _gated_residual(xc, delta, mods, base_gate, ctx):
    B, TT, D = xc.shape
    bt = _pick(ctx, (256, 128, 64))
    blk = pl.BlockSpec((1, bt, D), lambda b, j: (b, j, 0))
    return pl.pallas_call(
        _residual_kernel, grid=(B, TT // bt),
        in_specs=[blk, blk, pl.BlockSpec((1, 1, D), functools.partial(_mod_index(base_gate, ctx // bt), nb=B))],
        out_specs=blk, out_shape=jax.ShapeDtypeStruct((B, TT, D), F32),
        compiler_params=_cparams(("arbitrary", "arbitrary")),
    )(xc, delta, mods)


def _rope_tables(seq, ctx, d):
    nf = d // 4
    rows = seq // GRID_W
    row = jnp.broadcast_to(jnp.arange(rows, dtype=jnp.int32)[:, None], (rows, GRID_W)).reshape(seq)
    col = jnp.broadcast_to(jnp.arange(GRID_W, dtype=jnp.int32)[None, :], (rows, GRID_W)).reshape(seq)
    inv_freq = ROPE_BASE ** (-jnp.arange(nf, dtype=F32) / nf)
    ar = row.astype(F32)[:, None] * inv_freq
    ac = col.astype(F32)[:, None] * inv_freq
    cos = jnp.concatenate([jnp.cos(ar), jnp.cos(ar), jnp.cos(ac), jnp.cos(ac)], axis=-1)
    sin = jnp.concatenate([-jnp.sin(ar), jnp.sin(ar), -jnp.sin(ac), jnp.sin(ac)], axis=-1)
    cos = jnp.concatenate([jnp.ones((ctx, d), F32), cos], axis=0)
    sin = jnp.concatenate([jnp.zeros((ctx, d), F32), sin], axis=0)
    return cos, sin


def _rope(x, cos, sin, quarter):
    lane = lax.broadcasted_iota(jnp.int32, x.shape, 1)
    up = pltpu.roll(x, LANES - quarter, 1)
    down = pltpu.roll(x, quarter, 1)
    swapped = jnp.where(lane % (2 * quarter) < quarter, up, down)
    return x * cos + swapped * sin


def _prep_kernel(qb_ref, kb_ref, vb_ref, qd0_ref, qd1_ref, kd0_ref, kd1_ref, vd0_ref, vd1_ref,
                 cb_ref, sb_ref, cc_ref, sc_ref, oqb, okb, ovb, oqd, okd, ovd):
    cb, sb, cc, sc = cb_ref[...], sb_ref[...], cc_ref[...], sc_ref[...]
    qb, kb = qb_ref[0], kb_ref[0]
    for h in range(HB):
        hs = slice(h * LANES, (h + 1) * LANES)
        oqb[0, :, hs] = _rope(qb[:, hs], cb, sb, DKB // 4).astype(BF16)
        okb[0, :, hs] = (_rope(kb[:, hs], cb, sb, DKB // 4) * DKB ** -0.5).astype(BF16)
    ovb[0] = vb_ref[0].astype(BF16)
    for half, (qr, kr, vr) in enumerate(((qd0_ref, kd0_ref, vd0_ref), (qd1_ref, kd1_ref, vd1_ref))):
        q, k, v = qr[0], kr[0], vr[0]
        for h in range(WA // LANES):
            hs = slice(h * LANES, (h + 1) * LANES)
            head = half * (WA // LANES) + h
            oqd[0, head] = (_rope(q[:, hs], cc, sc, DHC // 4) * Q_SCALE_C).T.astype(BF16)
            okd[0, :, half * WA + h * LANES:half * WA + (h + 1) * LANES] = _rope(k[:, hs], cc, sc, DHC // 4).astype(BF16)
            ovd[0, head] = v[:, hs].T.astype(BF16)


def _prep(p, tabs, ctx):
    B, TT, _ = p.shape
    bt = _pick(ctx, (256, 128, 64))
    pspec = lambda c: pl.BlockSpec((1, bt, WA), lambda b, j: (b, j, c))
    tspec = pl.BlockSpec((bt, LANES), lambda b, j: (j, 0))
    o512 = pl.BlockSpec((1, bt, WA), lambda b, j: (b, j, 0))
    o1024 = pl.BlockSpec((1, bt, 2 * WA), lambda b, j: (b, j, 0))
    o_t = pl.BlockSpec((1, HC, LANES, bt), lambda b, j: (b, 0, 0, j))
    s512 = jax.ShapeDtypeStruct((B, TT, WA), BF16)
    s1024 = jax.ShapeDtypeStruct((B, TT, 2 * WA), BF16)
    s_t = jax.ShapeDtypeStruct((B, HC, LANES, TT), BF16)
    cols = (COL_QB, COL_KB, COL_VB, COL_QD, COL_QD + 1, COL_KD, COL_KD + 1, COL_VD, COL_VD + 1)
    return pl.pallas_call(
        _prep_kernel, grid=(B, TT // bt),
        in_specs=[pspec(c) for c in cols] + [tspec] * 4,
        out_specs=[o512, o512, o512, o_t, o1024, o_t],
        out_shape=[s512, s512, s512, s_t, s1024, s_t],
        compiler_params=_cparams(("arbitrary", "arbitrary")),
    )(*([p] * 9), *tabs)


def _hgrn_consts():
    C = CHUNK_A
    t = np.arange(C)[:, None]
    u = np.arange(C)[None, :]
    out = []
    for rev in (False, True):
        mats = [(u >= t) if rev else (u <= t)]
        for m in LEVELS_A:
            r = (t // (2 * m)) * (2 * m) + (m if rev else m - 1)
            mats.append((u >= r) if rev else (u <= r))
        out.append(np.concatenate(mats, axis=0))
    return jnp.asarray(np.stack(out).astype(np.float32), dtype=BF16)


def _hgrn_kernel(qf_ref, zf_ref, vf_ref, qb_ref, zb_ref, vb_ref, cm_ref, lb_ref, of_ref, ob_ref, s_scr):
    C = CHUNK_A

    @pl.when(pl.program_id(1) == 0)
    def _():
        s_scr[...] = jnp.zeros_like(s_scr)

    row = lax.broadcasted_iota(jnp.int32, (C, LANES), 0)
    rr = lax.broadcasted_iota(jnp.int32, (C, C), 0)
    cc = lax.broadcasted_iota(jnp.int32, (C, C), 1)
    blk_row = lax.broadcasted_iota(jnp.int32, (C // SUBLANES, SUBLANES, LANES), 1)
    ones = jnp.ones((C, LANES), BF16)
    q_refs, z_refs, v_refs, o_refs = (qf_ref, qb_ref), (zf_ref, zb_ref), (vf_ref, vb_ref), (of_ref, ob_ref)

    ks, bbs, bend_ts = [], [], []
    for d in range(2):
        z = z_refs[d][0]
        llb, l1m = lb_ref[d, 0:1], lb_ref[d, 1:2]
        log_sig = jnp.minimum(z, 0.0) - jnp.log1p(jnp.exp(-jnp.abs(z)))
        t2 = l1m + log_sig
        logf = jnp.maximum(llb, t2) + jnp.log1p(jnp.exp(-jnp.abs(llb - t2)))
        ks.append(1.0 - jnp.exp(logf))
        parts = _split3(logf)
        cm = cm_ref[d]
        bbs.append(_dot(cm, parts[0]) + (_dot(cm, parts[1]) + _dot(cm, parts[2])))
        bend_ts.append(_dot_tn(parts[0], ones) + (_dot_tn(parts[1], ones) + _dot_tn(parts[2], ones)))

    heads = [(d, h) for d in range(2) for h in range(HA)]

    def operands(d, h):
        hs = slice(h * LANES, (h + 1) * LANES)
        return q_refs[d][0, :, hs], ks[d][:, hs], v_refs[d][0, :, hs], bbs[d][:C, hs]

    level_dots = {}
    for d, h in heads:
        q, k, _, b = operands(d, h)
        dots = []
        for lvl, m in enumerate(LEVELS_A):
            bref = bbs[d][(lvl + 1) * C:(lvl + 2) * C, h * LANES:(h + 1) * LANES]
            e = jnp.exp(-jnp.abs(b - bref))
            upper = (row % (2 * m)) >= m
            q_side = jnp.logical_not(upper) if d else upper
            qt = jnp.where(q_side, q * e, 0.0).astype(BF16)
            kt = jnp.where(q_side, 0.0, k * e).astype(BF16)
            dots.append(_dot_nt(qt, kt))
        level_dots[d, h] = dots

    diag = {}
    for d, h in heads:
        q3, k3, v3, b3 = (x.reshape(C // SUBLANES, SUBLANES, LANES) for x in operands(d, h))
        acc = jnp.zeros_like(q3)
        for j in range(SUBLANES):
            kj, vj, bj = k3[:, j:j + 1], v3[:, j:j + 1], b3[:, j:j + 1]
            e = jnp.exp(jnp.minimum(b3 - bj, 0.0))
            a = jnp.sum(q3 * kj * e, axis=-1, keepdims=True)
            valid = (blk_row <= j) if d else (blk_row >= j)
            acc = acc + jnp.where(valid, a, 0.0) * vj
        diag[d, h] = acc.reshape(C, LANES)

    outs, updates = {}, {}
    for d, h in heads:
        q, k, v, b = operands(d, h)
        v16 = v.astype(BF16)
        s_old = s_scr[d, h]
        att = jnp.zeros((C, C), F32)
        for m, a in zip(LEVELS_A, level_dots[d, h]):
            att = att + jnp.where((rr // (2 * m)) == (cc // (2 * m)), a, 0.0)
        inter = _dot((q * jnp.exp(b)).astype(BF16), s_old.astype(BF16))
        outs[d, h] = inter + _dot(att.astype(BF16), v16)
        end = 0 if d else C - 1
        bend = b[end:end + 1, :]
        kdec = (k * jnp.exp(bend - b)).astype(BF16)
        updates[d, h] = jnp.exp(bend_ts[d][h * LANES:(h + 1) * LANES, :]) * s_old + _dot_tn(kdec, v16)

    for d, h in heads:
        o_refs[d][0, :, h * LANES:(h + 1) * LANES] = outs[d, h] + diag[d, h]
        s_scr[d, h] = updates[d, h]


def _bwd_chunk(i, nctx, n):
    return jnp.where(i < nctx, nctx - 1 - i, n - 1 + nctx - i)


def _hgrn(p, lb_tab, ctx):
    B, TT, _ = p.shape
    C = CHUNK_A
    n, nctx = TT // C, ctx // C
    fwd = lambda c: pl.BlockSpec((1, C, WA), lambda b, i: (b, i, c))
    bwd = lambda c: pl.BlockSpec((1, C, WA), lambda b, i: (b, _bwd_chunk(i, nctx, n), c))
    cm = _hgrn_consts()
    shape = jax.ShapeDtypeStruct((B, TT, WA), F32)
    return pl.pallas_call(
        _hgrn_kernel, grid=(B, n),
        in_specs=[fwd(COL_QA), fwd(COL_FF), fwd(COL_IA), bwd(COL_QA), bwd(COL_FB), bwd(COL_IA),
                  pl.BlockSpec(cm.shape, lambda b, i: (0, 0, 0)),
                  pl.BlockSpec((2, 2, WA), lambda b, i: (0, 0, 0))],
        out_specs=[pl.BlockSpec((1, C, WA), lambda b, i: (b, i, 0)),
                   pl.BlockSpec((1, C, WA), lambda b, i: (b, _bwd_chunk(i, nctx, n), 0))],
        out_shape=[shape, shape],
        scratch_shapes=[pltpu.VMEM((2, HA, DKA, DVA), F32)],
        compiler_params=_cparams(("arbitrary", "arbitrary")),
    )(p, p, p, p, p, p, cm, lb_tab)


def _ret_kernel(lg_ref, qf_ref, kf_ref, vf_ref, qb_ref, kb_ref, vb_ref, of_ref, ob_ref,
                s_scr, d_scr, rq_scr, rk_scr, gc_scr):
    C = CHUNK_B

    @pl.when((pl.program_id(0) == 0) & (pl.program_id(1) == 0))
    def _():
        t = lax.broadcasted_iota(jnp.int32, (C, C), 0)
        s = lax.broadcasted_iota(jnp.int32, (C, C), 1)
        pos = lax.broadcasted_iota(jnp.int32, (C, LANES), 0).astype(F32)
        for d in range(2):
            for h in range(HB):
                lg = lg_ref[d, h]
                delta = (s - t) if d else (t - s)
                d_scr[d, h] = jnp.where(delta >= 0, jnp.exp(jnp.maximum(delta, 0).astype(F32) * lg), 0.0)
                rq_scr[d, h] = jnp.exp(((C - pos) if d else (pos + 1.0)) * lg)
                rk_scr[d, h] = jnp.exp((pos if d else (C - 1.0 - pos)) * lg)
                gc_scr[d, h] = jnp.exp(jnp.full((SUBLANES, LANES), C, F32) * lg)

    @pl.when(pl.program_id(1) == 0)
    def _():
        s_scr[...] = jnp.zeros_like(s_scr)

    for d, (q_ref, k_ref, v_ref, o_ref) in enumerate(((qf_ref, kf_ref, vf_ref, of_ref), (qb_ref, kb_ref, vb_ref, ob_ref))):
        for h in range(HB):
            hs = slice(h * LANES, (h + 1) * LANES)
            q, k, v = q_ref[0, :, hs], k_ref[0, :, hs], v_ref[0, :, hs]
            att = (_dot_nt(q, k) * d_scr[d, h]).astype(BF16)
            s_old = s_scr[d, h]
            o_ref[0, :, hs] = _dot(att, v) + rq_scr[d, h] * _dot(q, s_old.astype(BF16))
            kdec = (k.astype(F32) * rk_scr[d, h]).astype(BF16)
            s_scr[d, h] = gc_scr[d, h][0:1, :] * s_old + _dot_tn(kdec, v)


def _retention(qb, kb, vb, log_gamma, ctx):
    B, TT, _ = qb.shape
    C = CHUNK_B
    n, nctx = TT // C, ctx // C
    fwd = pl.BlockSpec((1, C, WA), lambda b, i: (b, i, 0))
    bwd = pl.BlockSpec((1, C, WA), lambda b, i: (b, _bwd_chunk(i, nctx, n), 0))
    shape = jax.ShapeDtypeStruct((B, TT, WA), F32)
    return pl.pallas_call(
        _ret_kernel, grid=(B, n),
        in_specs=[pl.BlockSpec(memory_space=pltpu.SMEM), fwd, fwd, fwd, bwd, bwd, bwd],
        out_specs=[fwd, bwd], out_shape=[shape, shape],
        scratch_shapes=[pltpu.VMEM((2, HB, DKB, DVB), F32), pltpu.VMEM((2, HB, C, C), F32),
                        pltpu.VMEM((2, HB, C, LANES), F32), pltpu.VMEM((2, HB, C, LANES), F32),
                        pltpu.VMEM((2, HB, SUBLANES, LANES), F32)],
        compiler_params=_cparams(("arbitrary", "arbitrary")),
    )(log_gamma, qb, kb, vb, qb, kb, vb)


def _attn_kernel(lam_ref, q_ref, k_ref, v_ref, o_ref, qm_scr, m_scr, l_scr, acc_scr, *, ctx, tq, tk, nk):
    qi = pl.program_id(2)
    ki = pl.program_id(3)

    @pl.when(ki == 0)
    def _():
        q = q_ref[0, 0]
        row = lax.broadcasted_iota(jnp.int32, q.shape, 0)
        zero = jnp.zeros_like(q)
        qm_scr[0] = jnp.where(row < DHC, q, zero)
        qm_scr[1] = jnp.where(row < DHC, zero, q)
        m_scr[...] = jnp.full_like(m_scr, NEG_BIG)
        l_scr[...] = jnp.zeros_like(l_scr)
        acc_scr[...] = jnp.zeros_like(acc_scr)

    def step(masked):
        k = k_ref[0]
        vt = v_ref[0, 0]
        m_old, l_old, acc_old = m_scr[...], l_scr[...], acc_scr[...]
        key_subs = [slice(j, j + K_SUB) for j in range(0, tk, K_SUB)]
        m_out, l_out, acc_out = [[], []], [[], []], [[], []]
        chains = [(g, c) for g in range(tq // Q_GROUP) for c in range(2)]

        def scores(g, c):
            qg = qm_scr[c, :, g * Q_GROUP:(g + 1) * Q_GROUP]
            s = [_dot(k[ks], qg) for ks in key_subs]
            if masked and g * Q_GROUP < ctx:
                keys = lax.broadcasted_iota(jnp.int32, (tk, Q_GROUP), 0) + ki * tk
                qpos = lax.broadcasted_iota(jnp.int32, (tk, Q_GROUP), 1) + g * Q_GROUP
                hidden = (qpos < ctx) & (keys >= ctx)
                s = [jnp.where(hidden[ks], NEG_BIG, sj) for ks, sj in zip(key_subs, s)]
            return s

        def finish(g, c, s):
            gs = slice(g * Q_GROUP, (g + 1) * Q_GROUP)
            m_prev = m_old[c, :, gs]
            m_new = functools.reduce(jnp.maximum, [jnp.max(sj, axis=0, keepdims=True) for sj in s], m_prev)
            alpha = jnp.exp2(m_prev - m_new)
            p = [jnp.exp2(sj - m_new) for sj in s]
            l_new = functools.reduce(jnp.add, [jnp.sum(pj, axis=0, keepdims=True) for pj in p])
            pv = functools.reduce(jnp.add, [_dot(vt[:, ks], pj.astype(BF16)) for ks, pj in zip(key_subs, p)])
            l_out[c].append(alpha * l_old[c, :, gs] + l_new)
            acc_out[c].append(alpha * acc_old[c, :, gs] + pv)
            m_out[c].append(m_new)

        pending = [scores(*ch) for ch in chains[:SCORE_LOOKAHEAD]]
        for n, chain in enumerate(chains):
            if n + SCORE_LOOKAHEAD < len(chains):
                pending.append(scores(*chains[n + SCORE_LOOKAHEAD]))
            finish(*chain, pending.pop(0))
        for c in range(2):
            m_scr[c] = jnp.concatenate(m_out[c], axis=-1)
            l_scr[c] = jnp.concatenate(l_out[c], axis=-1)
            acc_scr[c] = jnp.concatenate(acc_out[c], axis=-1)

    @pl.when(qi == 0)
    def _():
        step(True)

    @pl.when(qi > 0)
    def _():
        step(False)

    @pl.when(ki == nk - 1)
    def _():
        o_t = acc_scr[0] / l_scr[0] - lam_ref[0] * (acc_scr[1] / l_scr[1])
        o_ref[0] = o_t.T


def _diff_attention(qd, kd, vd, lam, ctx):
    B, TT, _ = kd.shape
    tq = _pick(TT, (2816, 768, 512, 256))
    tk = _pick(TT, (768, 512, 256))
    assert ctx <= tk and ctx <= tq and tq % Q_GROUP == 0 and tk % K_SUB == 0
    nq, nk = TT // tq, TT // tk
    return pl.pallas_call(
        functools.partial(_attn_kernel, ctx=ctx, tq=tq, tk=tk, nk=nk),
        grid=(B, HC, nq, nk),
        in_specs=[pl.BlockSpec(memory_space=pltpu.SMEM),
                  pl.BlockSpec((1, 1, LANES, tq), lambda b, h, i, j: (b, h, 0, i)),
                  pl.BlockSpec((1, tk, LANES), lambda b, h, i, j: (b, j, h)),
                  pl.BlockSpec((1, 1, LANES, tk), lambda b, h, i, j: (b, h, 0, j))],
        out_specs=pl.BlockSpec((1, tq, LANES), lambda b, h, i, j: (b, i, h)),
        out_shape=jax.ShapeDtypeStruct((B, TT, HC * DVC), F32),
        scratch_shapes=[pltpu.VMEM((2, LANES, tq), BF16), pltpu.VMEM((2, 1, tq), F32),
                        pltpu.VMEM((2, 1, tq), F32), pltpu.VMEM((2, DVC, tq), F32)],
        compiler_params=_cparams(("arbitrary", "arbitrary", "arbitrary", "arbitrary")),
    )(lam, qd, kd, vd)


def _merge_kernel(af_ref, ab_ref, bf_ref, bb_ref, c_ref, ga_ref, gb_ref, na_ref, nb_ref, nc_ref, o_ref, *, c_scale):
    oa = af_ref[0] + ab_ref[0]
    ob = bf_ref[0] + bb_ref[0]
    oc = c_ref[0]
    ga, gb = ga_ref[0], gb_ref[0]
    for h in range(HA):
        hs = slice(h * LANES, (h + 1) * LANES)
        o_ref[0, :, hs] = (_rms(oa[:, hs]) * na_ref[:, hs] * _silu(ga[:, hs])).astype(o_ref.dtype)
    for h in range(HB):
        hs = slice(h * LANES, (h + 1) * LANES)
        os_ = slice(WA + h * LANES, WA + (h + 1) * LANES)
        o_ref[0, :, os_] = (_rms(ob[:, hs]) * nb_ref[:, hs] * _silu(gb[:, hs])).astype(o_ref.dtype)
    for h in range(HC):
        hs = slice(h * LANES, (h + 1) * LANES)
        os_ = slice(2 * WA + h * LANES, 2 * WA + (h + 1) * LANES)
        o_ref[0, :, os_] = (_rms(oc[:, hs]) * nc_ref[:, hs] * c_scale).astype(o_ref.dtype)


def _merge(oa_f, oa_b, ob_f, ob_b, oc, p, g_a, g_b, g_c, lam_init, ctx):
    B, TT, _ = oc.shape
    bt = _pick(ctx, (256, 128, 64))
    s512 = pl.BlockSpec((1, bt, WA), lambda b, j: (b, j, 0))
    pcol = lambda c: pl.BlockSpec((1, bt, WA), lambda b, j: (b, j, c))
    return pl.pallas_call(
        functools.partial(_merge_kernel, c_scale=1.0 - lam_init),
        grid=(B, TT // bt),
        in_specs=[s512, s512, s512, s512, pl.BlockSpec((1, bt, 2 * WA), lambda b, j: (b, j, 0)),
                  pcol(COL_GA), pcol(COL_GB),
                  pl.BlockSpec((1, WA), lambda b, j: (0, 0)), pl.BlockSpec((1, WA), lambda b, j: (0, 0)),
                  pl.BlockSpec((1, 2 * WA), lambda b, j: (0, 0))],
        out_specs=pl.BlockSpec((1, bt, D_MIX), lambda b, j: (b, j, 0)),
        out_shape=jax.ShapeDtypeStruct((B, TT, D_MIX), BF16),
        compiler_params=_cparams(("arbitrary", "arbitrary")),
    )(oa_f, oa_b, ob_f, ob_b, oc, p, p, g_a.reshape(1, -1), g_b.reshape(1, -1), g_c.reshape(1, -1))


def kernel(x, c, ctx, c_ctx, w_ada, b_ada, norm_mix, w_in, hgrn_lower_bounds, ret_decay, diff_lambda, norm_a, norm_b, norm_c, w_out, norm_ffn, ffn_w1, ffn_w3, ffn_w2, router, moe_w1, moe_w3, moe_w2, final_norm):
    B, S, D = x.shape
    CTX = ctx.shape[1]
    L = w_ada.shape[0]
    TT = CTX + S
    M = B * TT
    F = ffn_w1.shape[-1]
    assert w_in.shape[-1] == D_PROJ and w_out.shape[1] == D_MIX
    assert B + 1 <= SUBLANES and CTX % CHUNK_B == 0 and S % CHUNK_B == 0 and S % GRID_W == 0

    lbs = jnp.cumsum(jax.nn.softmax(hgrn_lower_bounds.astype(F32), axis=0), axis=0)
    lbs = lbs - lbs[0:1]
    lb_tabs = jnp.stack([jnp.log(lbs), jnp.log1p(-lbs)], axis=2)
    log_gammas = jnp.log1p(-jnp.exp2(-ret_decay.astype(F32)))
    tabs = _rope_tables(S, CTX, DKB) + _rope_tables(S, CTX, DHC)
    tabs = tabs[:2] + tuple(jnp.concatenate([t, t], axis=-1) for t in tabs[2:])

    cvec = jnp.zeros((SUBLANES, D), F32).at[:B].set(c).at[B].set(c_ctx)
    mods = _ada_all(cvec, w_ada, b_ada)
    mods = mods.reshape(L, SUBLANES, 6, D).transpose(0, 2, 1, 3).reshape(L * 6 * SUBLANES, 1, D)
    mod_base = lambda l, chunk: (l * 6 + chunk) * SUBLANES

    bm = _pick(TT, (768, 384, 256, 128))
    bn_in = _pick(D_PROJ, (1280, 768, 512))
    bn_out = _pick(D, (1024, 512, 256))
    bf = _pick(F, (512, 256, 128))
    bm_moe = _pick(M, (1056, 768, 384))
    bf_moe = _pick(F, (512, 256, 128))

    w_in16, w_out16 = w_in.astype(BF16), w_out.astype(BF16)
    ffn16 = tuple(w.astype(BF16) for w in (ffn_w1, ffn_w3, ffn_w2))
    moe16 = tuple(w.astype(BF16) for w in (moe_w1, moe_w3, moe_w2))

    xc = jnp.concatenate([ctx, x], axis=1)
    for l in range(L):
        lam_init = 0.8 - 0.6 * math.exp(-0.3 * l)
        lq1, lk1, lq2, lk2 = diff_lambda[l].astype(F32)
        lam = (jnp.exp(jnp.sum(lq1 * lk1)) - jnp.exp(jnp.sum(lq2 * lk2)) + lam_init).reshape(1)

        p = _modulated_matmul(xc.reshape(M, D), norm_mix[l], mods, mod_base(l, 0), mod_base(l, 1), w_in16, l,
                              B, CTX, bm, bn_in).reshape(B, TT, D_PROJ)
        oa_f, oa_b = _hgrn(p, lb_tabs[l], CTX)
        qb, kb, vb, qd, kd, vd = _prep(p, tabs, CTX)
        ob_f, ob_b = _retention(qb, kb, vb, log_gammas[l], CTX)
        oc = _diff_attention(qd, kd, vd, lam, CTX)
        y = _merge(oa_f, oa_b, ob_f, ob_b, oc, p, norm_a[l], norm_b[l], norm_c[l], lam_init, CTX)
        xc = _matmul_residual(y.reshape(M, D_MIX), w_out16, l, xc.reshape(M, D), mods,
                              mod_base(l, 2), B, CTX, bm, bn_out).reshape(B, TT, D)

        if l % 2 == 0:
            xc = _ffn(xc.reshape(M, D), norm_ffn[l], mods, mod_base(l, 3), mod_base(l, 4), mod_base(l, 5), *ffn16,
                      l // 2, B, CTX, bm, bf).reshape(B, TT, D)
        else:
            e = l // 2
            wr = jnp.zeros((D, LANES), F32).at[:, :N_EXPERTS].set(router[e])
            wr_hi = wr.astype(BF16)
            wr2 = jnp.stack([wr_hi, (wr - wr_hi.astype(F32)).astype(BF16)])
            h, gates = _modulate(xc, norm_ffn[l], mods, mod_base(l, 3), mod_base(l, 4), CTX, router=wr2)
            delta = _moe_sparse(h.reshape(M, D), gates.reshape(M, LANES), *moe16, e, bm_moe, bf_moe)
            xc = _gated_residual(xc, delta.reshape(B, TT, D), mods, mod_base(l, 5), CTX)

    zeros = jnp.zeros((SUBLANES, 1, D), F32)
    return _modulate(xc, final_norm, zeros, 0, 0, 0, out_dtype=F32, row_offset=CTX, rows=S)
```
